```python
import math
import jax, jax.numpy as jnp
from jax import lax
import numpy as np

D_MODEL = 1024
BATCH = 16
SEQ = 2048
DEPTH = 1

CHUNK = 64
Q_BLOCK = 128
NORM_EPS = 1e-6
NEG_INF = -1e30

DIFF_HEAD_DIM = 64
DIFF_HEADS = D_MODEL // (2 * DIFF_HEAD_DIM)
DIFF_V_DIM = 2 * DIFF_HEAD_DIM
DIFF_ROT_DIM = DIFF_HEAD_DIM // 4
ROPE_THETA = 500000.0

MLA_HEADS = 8
MLA_NOPE_DIM = 64
MLA_ROPE_DIM = 32
MLA_V_DIM = 64
MLA_Q_LORA = 3 * D_MODEL // 8
MLA_KV_LORA = D_MODEL // 4
MLA_ROPE_THETA = 10000.0

FFN_HIDDEN = ((8 * D_MODEL + 3 * 256 - 1) // (3 * 256)) * 256

PLE_DIM = 256

IN_SPLITS = (
    2 * DIFF_HEADS * DIFF_HEAD_DIM,
    2 * DIFF_HEADS * DIFF_HEAD_DIM,
    DIFF_HEADS * DIFF_V_DIM,
    MLA_Q_LORA,
    MLA_KV_LORA,
    MLA_ROPE_DIM,
    D_MODEL,
    D_MODEL,
)
IN_COLS = sum(IN_SPLITS)
IN_SPLIT_POINTS = [int(c) for c in np.cumsum(IN_SPLITS)[:-1]]

kernel_name = "hybrid_diffattn_mla_gated_block"


def rmsnorm(x, g):
    xf = x.astype(jnp.float32)
    y = xf * lax.rsqrt(jnp.mean(xf * xf, axis=-1, keepdims=True) + NORM_EPS)
    return (y * g.astype(jnp.float32)).astype(x.dtype)


def apply_rope(x, rot_dim, theta):
    seq = x.shape[1]
    half = rot_dim // 2
    pos = jnp.arange(seq, dtype=jnp.float32)
    inv_freq = theta ** (-(jnp.arange(0, rot_dim, 2, dtype=jnp.float32) / rot_dim))
    ang = pos[:, None] * inv_freq[None, :]
    cos = jnp.cos(ang)[None, :, None, :]
    sin = jnp.sin(ang)[None, :, None, :]
    xr = x[..., :rot_dim].astype(jnp.float32)
    x1, x2 = xr[..., :half], xr[..., half:]
    rot = jnp.concatenate([x1 * cos - x2 * sin, x2 * cos + x1 * sin], axis=-1).astype(x.dtype)
    return jnp.concatenate([rot, x[..., rot_dim:]], axis=-1)


def chunk_mask(q_start, seq):
    q_chunk = (q_start + jnp.arange(Q_BLOCK)) // CHUNK
    k_chunk = jnp.arange(seq) // CHUNK
    return k_chunk[None, :] <= q_chunk[:, None]


def masked_softmax(scores, mask):
    s = jnp.where(mask, scores.astype(jnp.float32), NEG_INF)
    return jax.nn.softmax(s, axis=-1)


def blocks_to_seq(o):
    nb, b, h, qb, d = o.shape
    return jnp.transpose(o, (1, 0, 3, 2, 4)).reshape(b, nb * qb, h, d)


def diff_attention(q1, q2, k1, k2, v, lam):
    seq = q1.shape[2]
    scale = DIFF_HEAD_DIM ** -0.5

    def block(i):
        s0 = i * Q_BLOCK
        mask = chunk_mask(s0, seq)
        qb1 = lax.dynamic_slice_in_dim(q1, s0, Q_BLOCK, axis=2)
        qb2 = lax.dynamic_slice_in_dim(q2, s0, Q_BLOCK, axis=2)
        a1 = masked_softmax(jnp.einsum('bhqd,bhkd->bhqk', qb1, k1).astype(jnp.float32) * scale, mask)
        a2 = masked_softmax(jnp.einsum('bhqd,bhkd->bhqk', qb2, k2).astype(jnp.float32) * scale, mask)
        w = (a1 - lam * a2).astype(v.dtype)
        return jnp.einsum('bhqk,bhkd->bhqd', w, v)

    return blocks_to_seq(lax.map(block, jnp.arange(seq // Q_BLOCK)))


def mla_attention(q_nope, q_rope, k_nope, k_rope, v):
    seq = q_nope.shape[2]
    scale = (MLA_NOPE_DIM + MLA_ROPE_DIM) ** -0.5

    def block(i):
        s0 = i * Q_BLOCK
        mask = chunk_mask(s0, seq)
        qn = lax.dynamic_slice_in_dim(q_nope, s0, Q_BLOCK, axis=2)
        qr = lax.dynamic_slice_in_dim(q_rope, s0, Q_BLOCK, axis=2)
        s = (jnp.einsum('bhqd,bhkd->bhqk', qn, k_nope).astype(jnp.float32)
             + jnp.einsum('bhqr,bkr->bhqk', qr, k_rope).astype(jnp.float32)) * scale
        a = masked_softmax(s, mask).astype(v.dtype)
        return jnp.einsum('bhqk,bhkd->bhqd', a, v)

    return blocks_to_seq(lax.map(block, jnp.arange(seq // Q_BLOCK)))


def setup_inputs(seed: int = 0) -> dict:
    key = jax.random.key(seed)
    ks = jax.random.split(key, 32)
    f32 = jnp.float32

    def nrm(k, shape, scale):
        return jax.random.normal(k, shape, f32) * scale

    def gain(k, shape):
        return 1.0 + 0.05 * jax.random.normal(k, shape, f32)

    L, D = DEPTH, D_MODEL
    return {
        "x": nrm(ks[0], (BATCH, SEQ, D), 1.0),
        "p": nrm(ks[1], (DEPTH, BATCH, SEQ, PLE_DIM), 1.0),
        "attn_norm": gain(ks[2], (L, D)),
        "w_in": nrm(ks[3], (L, D, IN_COLS), D ** -0.5),
        "b_gate": nrm(ks[4], (L, 2, D), 0.1),
        "lam_q1": nrm(ks[5], (L, DIFF_HEAD_DIM), 0.1),
        "lam_k1": nrm(ks[6], (L, DIFF_HEAD_DIM), 0.1),
        "lam_q2": nrm(ks[7], (L, DIFF_HEAD_DIM), 0.1),
        "lam_k2": nrm(ks[8], (L, DIFF_HEAD_DIM), 0.1),
        "diff_subln": gain(ks[9], (L, DIFF_V_DIM)),
        "w_o_diff": nrm(ks[10], (L, DIFF_HEADS * DIFF_V_DIM, D), (DIFF_HEADS * DIFF_V_DIM) ** -0.5),
        "q_norm": gain(ks[11], (L, MLA_Q_LORA)),
        "w_uq": nrm(ks[12], (L, MLA_Q_LORA, MLA_HEADS * (MLA_NOPE_DIM + MLA_ROPE_DIM)), MLA_Q_LORA ** -0.5),
        "kv_norm": gain(ks[13], (L, MLA_KV_LORA)),
        "w_ukv": nrm(ks[14], (L, MLA_KV_LORA, MLA_HEADS * (MLA_NOPE_DIM + MLA_V_DIM)), MLA_KV_LORA ** -0.5),
        "w_o_mla": nrm(ks[15], (L, MLA_HEADS * MLA_V_DIM, D), (MLA_HEADS * MLA_V_DIM) ** -0.5),
        "w_out": nrm(ks[16], (L, D, D), D ** -0.5),
        "ffn_norm": gain(ks[17], (L, D)),
        "w_ffn_gate": nrm(ks[18], (L, D, FFN_HIDDEN), D ** -0.5),
        "w_ffn_up": nrm(ks[19], (L, D, FFN_HIDDEN), D ** -0.5),
        "w_ffn_down": nrm(ks[20], (L, FFN_HIDDEN, D), FFN_HIDDEN ** -0.5),
        "ple_norm": gain(ks[21], (L, D)),
        "w_ple_gate": nrm(ks[22], (L, D, D), D ** -0.5),
        "b_ple_gate": nrm(ks[23], (L, D), 0.1),
        "w_ple": nrm(ks[24], (L, PLE_DIM, D), PLE_DIM ** -0.5),
        "final_norm": gain(ks[25], (D,)),
    }


def reference(x, p, attn_norm, w_in, b_gate, lam_q1, lam_k1, lam_q2, lam_k2, diff_subln,
              w_o_diff, q_norm, w_uq, kv_norm, w_ukv, w_o_mla, w_out, ffn_norm,
              w_ffn_gate, w_ffn_up, w_ffn_down, ple_norm, w_ple_gate, b_ple_gate, w_ple,
              final_norm):
    B, S, _ = x.shape
    for i in range(DEPTH):
        h = rmsnorm(x, attn_norm[i])
        proj = jnp.einsum('bsd,dc->bsc', h, w_in[i])
        dq, dk, dv, cq, ckv, kr, ga, gb = jnp.split(proj, IN_SPLIT_POINTS, axis=-1)

        dq = apply_rope(dq.reshape(B, S, 2 * DIFF_HEADS, DIFF_HEAD_DIM), DIFF_ROT_DIM, ROPE_THETA)
        dk = apply_rope(dk.reshape(B, S, 2 * DIFF_HEADS, DIFF_HEAD_DIM), DIFF_ROT_DIM, ROPE_THETA)
        dq = jnp.transpose(dq.reshape(B, S, DIFF_HEADS, 2, DIFF_HEAD_DIM), (3, 0, 2, 1, 4))
        dk = jnp.transpose(dk.reshape(B, S, DIFF_HEADS, 2, DIFF_HEAD_DIM), (3, 0, 2, 1, 4))
        dv = jnp.transpose(dv.reshape(B, S, DIFF_HEADS, DIFF_V_DIM), (0, 2, 1, 3))
        lam_init = 0.8 - 0.6 * math.exp(-0.3 * i)
        lam = (jnp.exp(jnp.sum(lam_q1[i].astype(jnp.float32) * lam_k1[i].astype(jnp.float32)))
               - jnp.exp(jnp.sum(lam_q2[i].astype(jnp.float32) * lam_k2[i].astype(jnp.float32)))
               + lam_init)
        od = diff_attention(dq[0], dq[1], dk[0], dk[1], dv, lam)
        od = rmsnorm(od, diff_subln[i]) * (1.0 - lam_init)
        out_a = jnp.einsum('bsc,cd->bsd', od.reshape(B, S, DIFF_HEADS * DIFF_V_DIM), w_o_diff[i])

        q = jnp.einsum('bsr,rc->bsc', rmsnorm(cq, q_norm[i]), w_uq[i])
        q = q.reshape(B, S, MLA_HEADS, MLA_NOPE_DIM + MLA_ROPE_DIM)
        q_nope = jnp.transpose(q[..., :MLA_NOPE_DIM], (0, 2, 1, 3))
        q_rope = jnp.transpose(apply_rope(q[..., MLA_NOPE_DIM:], MLA_ROPE_DIM, MLA_ROPE_THETA), (0, 2, 1, 3))
        kv = jnp.einsum('bsr,rc->bsc', rmsnorm(ckv, kv_norm[i]), w_ukv[i])
        kv = kv.reshape(B, S, MLA_HEADS, MLA_NOPE_DIM + MLA_V_DIM)
        k_nope = jnp.transpose(kv[..., :MLA_NOPE_DIM], (0, 2, 1, 3))
        v_mla = jnp.transpose(kv[..., MLA_NOPE_DIM:], (0, 2, 1, 3))
        k_rope = apply_rope(kr[:, :, None, :], MLA_ROPE_DIM, MLA_ROPE_THETA)[:, :, 0, :]
        om = mla_attention(q_nope, q_rope, k_nope, k_rope, v_mla)
        out_b = jnp.einsum('bsc,cd->bsd', om.reshape(B, S, MLA_HEADS * MLA_V_DIM), w_o_mla[i])

        merged = jax.nn.sigmoid(ga + b_gate[i, 0]) * out_a + jax.nn.sigmoid(gb + b_gate[i, 1]) * out_b
        x = x + jnp.einsum('bsd,de->bse', merged, w_out[i])

        h = rmsnorm(x, ffn_norm[i])
        hid = jax.nn.silu(jnp.einsum('bsd,df->bsf', h, w_ffn_gate[i])) * jnp.einsum('bsd,df->bsf', h, w_ffn_up[i])
        x = x + jnp.einsum('bsf,fd->bsd', hid, w_ffn_down[i])

        h = rmsnorm(x, ple_norm[i])
        gate = jax.nn.sigmoid(jnp.einsum('bsd,de->bse', h, w_ple_gate[i]) + b_ple_gate[i])
        x = x + jnp.einsum('bsp,pd->bsd', p[i], w_ple[i]) * gate

    return rmsnorm(x, final_norm)
```

```python
import functools
import math

import jax
import jax.numpy as jnp
from jax import lax
from jax.experimental import pallas as pl
from jax.experimental.pallas import tpu as pltpu

D_MODEL = 1024
CHUNK = 64
NORM_EPS = 1e-6
NEG_INF = -1e30

DIFF_HEAD_DIM = 64
DIFF_HEADS = 8
DIFF_V_DIM = 128
DIFF_ROT_DIM = 16
ROPE_THETA = 500000.0
LAM_INIT = 0.8 - 0.6 * math.exp(-0.3 * 0)

MLA_HEADS = 8
MLA_NOPE_DIM = 64
MLA_ROPE_DIM = 32
MLA_V_DIM = 64
MLA_Q_LORA = 384
MLA_KV_LORA = 256
MLA_ROPE_THETA = 10000.0

FFN_HIDDEN = 2816
PLE_DIM = 256

LANES = 128
LOG2E = math.log2(math.e)
VMEM_LIMIT_BYTES = 56 * 1024 * 1024

ROW_TILE_PROJ = 256
ROW_TILE_MLP = 256
Q_TILE = 256

BF16 = jnp.bfloat16
F32 = jnp.float32


def _rmsnorm(x, g):
    ms = jnp.mean(x * x, axis=-1, keepdims=True)
    return x * lax.rsqrt(ms + NORM_EPS) * g


def _dot(a, b):
    return jnp.dot(a, b, preferred_element_type=F32)


def _dot_nt(a, b):
    return lax.dot_general(a, b, (((1,), (1,)), ((), ())), preferred_element_type=F32)


def _sigmoid(x):
    return 1.0 / (1.0 + jnp.exp(-x))


def _rope_block(x, c, s_up, s_dn, shift):
    return (x * c + pltpu.roll(x, shift, 1) * s_up
            + pltpu.roll(x, LANES - shift, 1) * s_dn)


def _proj_kernel(x_ref, an_ref, wqkv_ref, wc_ref, qn_ref, wuq_ref, kvn_ref, wukv_ref,
                 dq_tab, dk_tab, mq_tab, mk_tab,
                 dq_ref, dk_ref, dv_ref, mq_ref, mk_ref, mv_ref):
    h = _rmsnorm(x_ref[...], an_ref[...]).astype(BF16)

    qkv = _dot(h, wqkv_ref[...])
    half = DIFF_ROT_DIM // 2
    for blk in range(DIFF_HEADS):
        lo = blk * LANES
        dq_ref[:, lo:lo + LANES] = _rope_block(
            qkv[:, lo:lo + LANES], dq_tab[0], dq_tab[1], dq_tab[2], half).astype(BF16)
        dk_ref[:, lo:lo + LANES] = _rope_block(
            qkv[:, D_MODEL + lo:D_MODEL + lo + LANES], dk_tab[0], dk_tab[1], dk_tab[2],
            half).astype(BF16)
    dv_ref[...] = qkv[:, 2 * D_MODEL:].astype(BF16)

    c = _dot(h, wc_ref[...])
    cq = _rmsnorm(c[:, :MLA_Q_LORA], qn_ref[...]).astype(BF16)
    ckv = _rmsnorm(c[:, MLA_Q_LORA:MLA_Q_LORA + MLA_KV_LORA], kvn_ref[...]).astype(BF16)
    kr = c[:, MLA_Q_LORA + MLA_KV_LORA:]
    mhalf = MLA_ROPE_DIM // 2
    kr = _rope_block(kr, mk_tab[0], mk_tab[1], mk_tab[2], mhalf)

    q = _dot(cq, wuq_ref[...])
    kv = _dot(ckv, wukv_ref[...])
    for blk in range(MLA_HEADS):
        lo = blk * LANES
        mq_ref[:, lo:lo + LANES] = _rope_block(
            q[:, lo:lo + LANES], mq_tab[0], mq_tab[1], mq_tab[2], mhalf).astype(BF16)
        mk_ref[:, lo:lo + LANES] = (kv[:, lo:lo + LANES] + kr).astype(BF16)
    mv_ref[...] = kv[:, MLA_HEADS * LANES:].astype(BF16)


def _diag_mask(tq):
    r = lax.broadcasted_iota(jnp.int32, (tq, tq), 0) // CHUNK
    c = lax.broadcasted_iota(jnp.int32, (tq, tq), 1) // CHUNK
    return c <= r


def _softmax_parts(s_off, s_diag):
    m = jnp.max(s_diag, axis=-1, keepdims=True)
    if s_off is not None:
        m = jnp.maximum(m, jnp.max(s_off, axis=-1, keepdims=True))
    e_diag = jnp.exp2(s_diag - m)
    l = jnp.sum(e_diag, axis=-1, keepdims=True)
    e_off = None
    if s_off is not None:
        e_off = jnp.exp2(s_off - m)
        l = l + jnp.sum(e_off, axis=-1, keepdims=True)
    return e_off, e_diag, l


def _diff_attn_kernel(lq1_ref, lk1_ref, lq2_ref, lk2_ref, subln_ref, q_ref, k_ref, v_ref, o_ref,
                      *, seq, tq):
    lam = (jnp.exp(jnp.sum(lq1_ref[...] * lk1_ref[...], axis=-1, keepdims=True))
           - jnp.exp(jnp.sum(lq2_ref[...] * lk2_ref[...], axis=-1, keepdims=True))
           + LAM_INIT)
    lane = lax.broadcasted_iota(jnp.int32, (tq, LANES), 1)
    first_map = lane < DIFF_HEAD_DIM
    mask = _diag_mask(tq)
    zero = jnp.zeros((), BF16)
    for i in range(seq // tq):
        lo = i * tq
        qi = q_ref[lo:lo + tq, :]
        q1 = jnp.where(first_map, qi, zero)
        q2 = jnp.where(first_map, zero, qi)
        k_diag = k_ref[lo:lo + tq, :]
        s1_diag = jnp.where(mask, _dot_nt(q1, k_diag), NEG_INF)
        s2_diag = jnp.where(mask, _dot_nt(q2, k_diag), NEG_INF)
        s1_off = s2_off = None
        if i > 0:
            k_off = k_ref[0:lo, :]
            s1_off = _dot_nt(q1, k_off)
            s2_off = _dot_nt(q2, k_off)
        e1_off, e1_diag, l1 = _softmax_parts(s1_off, s1_diag)
        e2_off, e2_diag, l2 = _softmax_parts(s2_off, s2_diag)
        r1 = 1.0 / l1
        r2 = lam / l2
        w_diag = (e1_diag * r1 - e2_diag * r2).astype(BF16)
        o = _dot(w_diag, v_ref[lo:lo + tq, :])
        if i > 0:
            w_off = (e1_off * r1 - e2_off * r2).astype(BF16)
            o = o + _dot(w_off, v_ref[0:lo, :])
        o = _rmsnorm(o, subln_ref[...]) * (1.0 - LAM_INIT)
        o_ref[lo:lo + tq, :] = o.astype(BF16)


def _mla_attn_kernel(q_ref, k_ref, v_ref, o_ref, *, seq, tq):
    lane = lax.broadcasted_iota(jnp.int32, (tq, LANES), 1)
    first_head = lane < MLA_V_DIM
    mask = _diag_mask(tq)
    for i in range(seq // tq):
        lo = i * tq
        outs = []
        for j in range(2):
            cl = j * LANES
            qj = q_ref[lo:lo + tq, cl:cl + LANES]
            s_diag = jnp.where(mask, _dot_nt(qj, k_ref[lo:lo + tq, cl:cl + LANES]), NEG_INF)
            s_off = _dot_nt(qj, k_ref[0:lo, cl:cl + LANES]) if i > 0 else None
            e_off, e_diag, l = _softmax_parts(s_off, s_diag)
            o = _dot(e_diag.astype(BF16), v_ref[lo:lo + tq, :])
            if i > 0:
                o = o + _dot(e_off.astype(BF16), v_ref[0:lo, :])
            outs.append(o * (1.0 / l))
        o_ref[lo:lo + tq, :] = jnp.where(first_head, outs[0], outs[1]).astype(BF16)


def _mlp_kernel(x_ref, od_ref, om_ref, p_ref, an_ref, wg_ref, bg_ref, wod_ref, wom_ref, wout_ref,
                fn_ref, wfg_ref, wfu_ref, wfd_ref, pn_ref, wpg_ref, bpg_ref, wple_ref, final_ref,
                o_ref):
    x = x_ref[...]
    h = _rmsnorm(x, an_ref[...]).astype(BF16)
    gates = _sigmoid(_dot(h, wg_ref[...]) + bg_ref[...])
    out_a = _dot(od_ref[...], wod_ref[...])
    out_b = _dot(om_ref[...], wom_ref[...])
    merged = gates[:, :D_MODEL] * out_a + gates[:, D_MODEL:] * out_b
    x = x + _dot(merged.astype(BF16), wout_ref[...])

    h = _rmsnorm(x, fn_ref[...]).astype(BF16)
    gt = _dot(h, wfg_ref[...])
    up = _dot(h, wfu_ref[...])
    hid = (gt * _sigmoid(gt) * up).astype(BF16)
    x = x + _dot(hid, wfd_ref[...])

    h = _rmsnorm(x, pn_ref[...]).astype(BF16)
    gate = _sigmoid(_dot(h, wpg_ref[...]) + bpg_ref[...])
    x = x + _dot(p_ref[...].astype(BF16), wple_ref[...]) * gate

    o_ref[...] = _rmsnorm(x, final_ref[...])


def _rope_tables(seq, rot_dim, theta, lane_start, period, scale, passthrough):
    half = rot_dim // 2
    pos = jnp.arange(seq, dtype=F32)
    inv_freq = theta ** (-(jnp.arange(0, rot_dim, 2, dtype=F32) / rot_dim))
    ang = pos[:, None] * inv_freq[None, :]
    cos, sin = jnp.cos(ang), jnp.sin(ang)
    lane = jnp.arange(LANES) % period - lane_start
    idx = jnp.clip(lane, 0, rot_dim - 1) % half
    cos_l = jnp.take(cos, idx, axis=1)
    sin_l = jnp.take(sin, idx, axis=1)
    in_lo = (lane >= 0) & (lane < half)
    in_hi = (lane >= half) & (lane < rot_dim)
    c = jnp.where(in_lo | in_hi, cos_l, passthrough)
    s_up = jnp.where(in_hi, sin_l, 0.0)
    s_dn = jnp.where(in_lo, -sin_l, 0.0)
    return jnp.stack([c, s_up, s_dn]).astype(F32) * scale


def _const_spec(shape):
    nd = len(shape)
    return pl.BlockSpec(shape, lambda *_: (0,) * nd, pipeline_mode=pl.Buffered(1))


def _params(*sem):
    return pltpu.CompilerParams(dimension_semantics=sem, vmem_limit_bytes=VMEM_LIMIT_BYTES)


def kernel(x, p, attn_norm, w_in, b_gate, lam_q1, lam_k1, lam_q2, lam_k2, diff_subln, w_o_diff,
           q_norm, w_uq, kv_norm, w_ukv, w_o_mla, w_out, ffn_norm, w_ffn_gate, w_ffn_up,
           w_ffn_down, ple_norm, w_ple_gate, b_ple_gate, w_ple, final_norm):
    B, S, D = x.shape
    T = B * S
    assert D == D_MODEL and w_in.shape[0] == 1
    x2 = x.reshape(T, D)
    p2 = p[0].reshape(T, PLE_DIM)

    wi = w_in[0]
    o_cq = 3 * D
    o_ckv = o_cq + MLA_Q_LORA
    o_kr = o_ckv + MLA_KV_LORA
    o_g = o_kr + MLA_ROPE_DIM
    w_qkv = wi[:, :o_cq].astype(BF16)
    zeros = lambda n: jnp.zeros((D, n), F32)
    w_c = jnp.concatenate(
        [wi[:, o_cq:o_kr], zeros(MLA_NOPE_DIM), wi[:, o_kr:o_g],
         zeros(LANES - MLA_NOPE_DIM - MLA_ROPE_DIM)], axis=1).astype(BF16)
    w_g = wi[:, o_g:].astype(BF16)
    b_g = b_gate[0].reshape(1, 2 * D)

    wuq = w_uq[0].reshape(MLA_Q_LORA, MLA_HEADS, MLA_NOPE_DIM + MLA_ROPE_DIM)
    wuq = jnp.pad(wuq, ((0, 0), (0, 0), (0, LANES - MLA_NOPE_DIM - MLA_ROPE_DIM)))
    wuq = wuq.reshape(MLA_Q_LORA, MLA_HEADS * LANES).astype(BF16)
    wukv = w_ukv[0].reshape(MLA_KV_LORA, MLA_HEADS, MLA_NOPE_DIM + MLA_V_DIM)
    wk = jnp.pad(wukv[:, :, :MLA_NOPE_DIM], ((0, 0), (0, 0), (0, LANES - MLA_NOPE_DIM)))
    wv = wukv[:, :, MLA_NOPE_DIM:]
    wukv = jnp.concatenate([wk.reshape(MLA_KV_LORA, MLA_HEADS * LANES),
                            wv.reshape(MLA_KV_LORA, MLA_HEADS * MLA_V_DIM)], axis=1).astype(BF16)

    d_scale = DIFF_HEAD_DIM ** -0.5 * LOG2E
    m_scale = (MLA_NOPE_DIM + MLA_ROPE_DIM) ** -0.5 * LOG2E
    dq_tab = _rope_tables(S, DIFF_ROT_DIM, ROPE_THETA, 0, DIFF_HEAD_DIM, d_scale, 1.0)
    dk_tab = _rope_tables(S, DIFF_ROT_DIM, ROPE_THETA, 0, DIFF_HEAD_DIM, 1.0, 1.0)
    mq_tab = _rope_tables(S, MLA_ROPE_DIM, MLA_ROPE_THETA, MLA_NOPE_DIM, LANES, m_scale, 1.0)
    mq_tab = mq_tab * (jnp.arange(LANES) < MLA_NOPE_DIM + MLA_ROPE_DIM)
    mk_tab = _rope_tables(S, MLA_ROPE_DIM, MLA_ROPE_THETA, MLA_NOPE_DIM, LANES, 1.0, 0.0)

    row = lambda a: a.reshape(1, -1)

    tm = ROW_TILE_PROJ
    tiles_per_seq = S // tm
    tab_spec = pl.BlockSpec((3, tm, LANES), lambda i: (0, i % tiles_per_seq, 0))
    act = lambda n: pl.BlockSpec((tm, n), lambda i: (i, 0))
    dq, dk, dv, mq, mk, mv = pl.pallas_call(
        _proj_kernel,
        grid=(T // tm,),
        in_specs=[act(D), _const_spec((1, D)), _const_spec(w_qkv.shape), _const_spec(w_c.shape),
                  _const_spec((1, MLA_Q_LORA)), _const_spec(wuq.shape),
                  _const_spec((1, MLA_KV_LORA)), _const_spec(wukv.shape),
                  tab_spec, tab_spec, tab_spec, tab_spec],
        out_specs=[act(D), act(D), act(D), act(D), act(D), act(MLA_HEADS * MLA_V_DIM)],
        out_shape=[jax.ShapeDtypeStruct((T, D), BF16)] * 5
                  + [jax.ShapeDtypeStruct((T, MLA_HEADS * MLA_V_DIM), BF16)],
        compiler_params=_params("parallel"),
        name="proj",
    )(x2, row(attn_norm[0]), w_qkv, w_c, row(q_norm[0]), wuq, row(kv_norm[0]), wukv,
      dq_tab, dk_tab, mq_tab, mk_tab)

    head_blk = pl.BlockSpec((S, LANES), lambda b, h: (b, h))
    lam_spec = _const_spec((1, DIFF_HEAD_DIM))
    od = pl.pallas_call(
        functools.partial(_diff_attn_kernel, seq=S, tq=Q_TILE),
        grid=(B, DIFF_HEADS),
        in_specs=[lam_spec, lam_spec, lam_spec, lam_spec, _const_spec((1, DIFF_V_DIM)),
                  head_blk, head_blk, head_blk],
        out_specs=head_blk,
        out_shape=jax.ShapeDtypeStruct((T, DIFF_HEADS * DIFF_V_DIM), BF16),
        compiler_params=_params("parallel", "parallel"),
        name="diff_attn",
    )(lam_q1, lam_k1, lam_q2, lam_k2, row(diff_subln[0]), dq, dk, dv)

    pair_blk = pl.BlockSpec((S, 2 * LANES), lambda b, g: (b, g))
    om = pl.pallas_call(
        functools.partial(_mla_attn_kernel, seq=S, tq=Q_TILE),
        grid=(B, MLA_HEADS // 2),
        in_specs=[pair_blk, pair_blk, head_blk],
        out_specs=head_blk,
        out_shape=jax.ShapeDtypeStruct((T, MLA_HEADS * MLA_V_DIM), BF16),
        compiler_params=_params("parallel", "parallel"),
        name="mla_attn",
    )(mq, mk, mv)

    tm = ROW_TILE_MLP
    act = lambda n: pl.BlockSpec((tm, n), lambda i: (i, 0))
    weights = [w_g, b_g, w_o_diff[0].astype(BF16), w_o_mla[0].astype(BF16), w_out[0].astype(BF16),
               row(ffn_norm[0]), w_ffn_gate[0].astype(BF16), w_ffn_up[0].astype(BF16),
               w_ffn_down[0].astype(BF16), row(ple_norm[0]), w_ple_gate[0].astype(BF16),
               row(b_ple_gate[0]), w_ple[0].astype(BF16), row(final_norm)]
    out = pl.pallas_call(
        _mlp_kernel,
        grid=(T // tm,),
        in_specs=[act(D), act(D), act(MLA_HEADS * MLA_V_DIM), act(PLE_DIM), _const_spec((1, D))]
                 + [_const_spec(w.shape) for w in weights],
        out_specs=act(D),
        out_shape=jax.ShapeDtypeStruct((T, D), F32),
        compiler_params=_params("parallel"),
        name="mlp",
    )(x2, od, om, p2, row(attn_norm[0]), *weights)
    return out.reshape(B, S, D)
```

```python
import functools
import math

import jax
import jax.numpy as jnp
from jax import lax
from jax.experimental import pallas as pl
from jax.experimental.pallas import tpu as pltpu

D_MODEL = 1024
CHUNK = 64
NORM_EPS = 1e-6
NEG_INF = -1e30

DIFF_HEAD_DIM = 64
DIFF_HEADS = 8
DIFF_V_DIM = 128
DIFF_ROT_DIM = 16
ROPE_THETA = 500000.0
LAM_INIT = 0.8 - 0.6 * math.exp(-0.3 * 0)

MLA_HEADS = 8
MLA_NOPE_DIM = 64
MLA_ROPE_DIM = 32
MLA_V_DIM = 64
MLA_Q_LORA = 384
MLA_KV_LORA = 256
MLA_ROPE_THETA = 10000.0

FFN_HIDDEN = 2816
PLE_DIM = 256

LANES = 128
ONES_ROWS = 16
LOG2E = math.log2(math.e)
VMEM_LIMIT_BYTES = 56 * 1024 * 1024

ROW_TILE_PROJ = 256
ROW_TILE_MLP = 256
Q_TILE = 256
SCORE_LOOKAHEAD = 6

BF16 = jnp.bfloat16
F32 = jnp.float32


def _rmsnorm(x, g):
    ms = jnp.mean(x * x, axis=-1, keepdims=True)
    return x * lax.rsqrt(ms + NORM_EPS) * g


def _dot(a, b):
    return jnp.dot(a, b, preferred_element_type=F32)


def _dot_nt(a, b):
    return lax.dot_general(a, b, (((1,), (1,)), ((), ())), preferred_element_type=F32)


def _sigmoid(x):
    return 1.0 / (1.0 + jnp.exp(-x))


def _rope_block(x, c, s_up, s_dn, shift):
    return (x * c + pltpu.roll(x, shift, 1) * s_up
            + pltpu.roll(x, LANES - shift, 1) * s_dn)


def _proj_kernel(x_ref, an_ref, wqkv_ref, wc_ref, qn_ref, wuq_ref, kvn_ref, wukv_ref,
                 dq_tab, dk_tab, mq_tab, mk_tab,
                 dq_ref, dk_ref, dv_ref, mq_ref, mk_ref, mv_ref):
    h = _rmsnorm(x_ref[...], an_ref[...]).astype(BF16)

    qkv = _dot(h, wqkv_ref[...])
    half = DIFF_ROT_DIM // 2
    for blk in range(DIFF_HEADS):
        lo = blk * LANES
        dq_ref[:, lo:lo + LANES] = _rope_block(
            qkv[:, lo:lo + LANES], dq_tab[0], dq_tab[1], dq_tab[2], half).astype(BF16)
        dk_ref[:, lo:lo + LANES] = _rope_block(
            qkv[:, D_MODEL + lo:D_MODEL + lo + LANES], dk_tab[0], dk_tab[1], dk_tab[2],
            half).astype(BF16)
    dv_ref[...] = qkv[:, 2 * D_MODEL:].astype(BF16)

    c = _dot(h, wc_ref[...])
    cq = _rmsnorm(c[:, :MLA_Q_LORA], qn_ref[...]).astype(BF16)
    ckv = _rmsnorm(c[:, MLA_Q_LORA:MLA_Q_LORA + MLA_KV_LORA], kvn_ref[...]).astype(BF16)
    kr = c[:, MLA_Q_LORA + MLA_KV_LORA:]
    mhalf = MLA_ROPE_DIM // 2
    kr = _rope_block(kr, mk_tab[0], mk_tab[1], mk_tab[2], mhalf)

    q = _dot(cq, wuq_ref[...])
    kv = _dot(ckv, wukv_ref[...])
    for blk in range(MLA_HEADS):
        lo = blk * LANES
        mq_ref[:, lo:lo + LANES] = _rope_block(
            q[:, lo:lo + LANES], mq_tab[0], mq_tab[1], mq_tab[2], mhalf).astype(BF16)
        mk_ref[:, lo:lo + LANES] = (kv[:, lo:lo + LANES] + kr).astype(BF16)
    mv_ref[...] = kv[:, MLA_HEADS * LANES:].astype(BF16)


def _pair_mask_t(tq):
    key = lax.broadcasted_iota(jnp.int32, (tq, 2 * tq), 0) // CHUNK
    qry = (lax.broadcasted_iota(jnp.int32, (tq, 2 * tq), 1) % tq) // CHUNK
    return key <= qry


def _block_diag_q(q, split):
    lane = lax.broadcasted_iota(jnp.int32, q.shape, 1)
    zero = jnp.zeros((), q.dtype)
    return jnp.concatenate([jnp.where(lane < split, q, zero), jnp.where(lane < split, zero, q)],
                           axis=0)


def _attend_tiles(k_ref, q_ref, split, vt_ref, s_ref, seq, tq, finish):
    n = seq // tq
    mask_t = _pair_mask_t(tq)
    chunks = [(t, kb) for t in range(n) for kb in range(t + 1)]
    q_bd = {}

    def scores(c):
        t, kb = chunks[c]
        if t not in q_bd:
            q_bd[t] = _block_diag_q(q_ref[t * tq:(t + 1) * tq, :], split)
        s = _dot_nt(k_ref[kb * tq:(kb + 1) * tq, :], q_bd[t])
        s_ref[c % nslot] = jnp.where(mask_t, s, NEG_INF) if kb == t else s

    nslot = s_ref.shape[0]
    for c in range(nslot - 1):
        scores(c)
    run_max = acc = None
    for c, (t, kb) in enumerate(chunks):
        if c + nslot - 1 < len(chunks):
            scores(c + nslot - 1)
        m = jnp.max(s_ref[c % nslot], axis=0, keepdims=True)
        new_max = m if kb == 0 else jnp.maximum(run_max, m)
        e = jnp.exp2((s_ref[c % nslot] - new_max).astype(BF16))
        part = _dot(vt_ref[:, kb * tq:(kb + 1) * tq], e)
        acc = part if kb == 0 else acc * jnp.exp2(run_max - new_max) + part
        run_max = new_max
        if kb == t:
            finish(t * tq, acc)


def _fill_vt(vt_ref, v_ref):
    width = v_ref.shape[1]
    vt_ref[0:width, :] = v_ref[...].T
    vt_ref[width:, :] = jnp.ones((vt_ref.shape[0] - width, vt_ref.shape[1]), BF16)


def _diff_attn_kernel(lq1_ref, lk1_ref, lq2_ref, lk2_ref, subln_ref, q_ref, k_ref, v_ref, o_ref,
                      vt_ref, s_ref, *, seq, tq):
    lam = (jnp.exp(jnp.sum(lq1_ref[...] * lk1_ref[...], axis=-1, keepdims=True))
           - jnp.exp(jnp.sum(lq2_ref[...] * lk2_ref[...], axis=-1, keepdims=True))
           + LAM_INIT)
    dv = DIFF_V_DIM
    _fill_vt(vt_ref, v_ref)

    def finish(lo, a):
        o_t = (a[0:dv, 0:tq] * (1.0 / a[dv:dv + 1, 0:tq])
               - a[0:dv, tq:] * (lam / a[dv:dv + 1, tq:]))
        o = _rmsnorm(o_t.T, subln_ref[...]) * (1.0 - LAM_INIT)
        o_ref[lo:lo + tq, :] = o.astype(BF16)

    _attend_tiles(k_ref, q_ref, DIFF_HEAD_DIM, vt_ref, s_ref, seq, tq, finish)


def _mla_attn_kernel(q_ref, k_ref, v_ref, o_ref, vt_ref, s_ref, *, seq, tq):
    dv = MLA_V_DIM
    _fill_vt(vt_ref, v_ref)

    def finish(lo, a):
        o_t = jnp.concatenate(
            [a[0:dv, 0:tq] * (1.0 / a[2 * dv:2 * dv + 1, 0:tq]),
             a[dv:2 * dv, tq:] * (1.0 / a[2 * dv:2 * dv + 1, tq:])], axis=0)
        o_ref[lo:lo + tq, :] = o_t.T.astype(BF16)

    _attend_tiles(k_ref, q_ref, LANES, vt_ref, s_ref, seq, tq, finish)


def _mlp_kernel(x_ref, od_ref, om_ref, p_ref, an_ref, wg_ref, bg_ref, wod_ref, wom_ref, wout_ref,
                fn_ref, wfg_ref, wfu_ref, wfd_ref, pn_ref, wpg_ref, bpg_ref, wple_ref, final_ref,
                o_ref):
    x = x_ref[...]
    h = _rmsnorm(x, an_ref[...]).astype(BF16)
    gates = _sigmoid(_dot(h, wg_ref[...]) + bg_ref[...])
    out_a = _dot(od_ref[...], wod_ref[...])
    out_b = _dot(om_ref[...], wom_ref[...])
    merged = gates[:, :D_MODEL] * out_a + gates[:, D_MODEL:] * out_b
    x = x + _dot(merged.astype(BF16), wout_ref[...])

    h = _rmsnorm(x, fn_ref[...]).astype(BF16)
    gt = _dot(h, wfg_ref[...])
    up = _dot(h, wfu_ref[...])
    hid = (gt * _sigmoid(gt) * up).astype(BF16)
    x = x + _dot(hid, wfd_ref[...])

    h = _rmsnorm(x, pn_ref[...]).astype(BF16)
    gate = _sigmoid(_dot(h, wpg_ref[...]) + bpg_ref[...])
    x = x + _dot(p_ref[...].astype(BF16), wple_ref[...]) * gate

    o_ref[...] = _rmsnorm(x, final_ref[...])


def _rope_tables(seq, rot_dim, theta, lane_start, period, scale, passthrough):
    half = rot_dim // 2
    pos = jnp.arange(seq, dtype=F32)
    inv_freq = theta ** (-(jnp.arange(0, rot_dim, 2, dtype=F32) / rot_dim))
    ang = pos[:, None] * inv_freq[None, :]
    cos, sin = jnp.cos(ang), jnp.sin(ang)
    lane = jnp.arange(LANES) % period - lane_start
    idx = jnp.clip(lane, 0, rot_dim - 1) % half
    cos_l = jnp.take(cos, idx, axis=1)
    sin_l = jnp.take(sin, idx, axis=1)
    in_lo = (lane >= 0) & (lane < half)
    in_hi = (lane >= half) & (lane < rot_dim)
    c = jnp.where(in_lo | in_hi, cos_l, passthrough)
    s_up = jnp.where(in_hi, sin_l, 0.0)
    s_dn = jnp.where(in_lo, -sin_l, 0.0)
    return jnp.stack([c, s_up, s_dn]).astype(F32) * scale


def _const_spec(shape):
    nd = len(shape)
    return pl.BlockSpec(shape, lambda *_: (0,) * nd, pipeline_mode=pl.Buffered(1))


def _params(*sem, flags=None):
    return pltpu.CompilerParams(dimension_semantics=sem, vmem_limit_bytes=VMEM_LIMIT_BYTES,
                                flags=flags)


ATTN_FLAGS = None


def kernel(x, p, attn_norm, w_in, b_gate, lam_q1, lam_k1, lam_q2, lam_k2, diff_subln, w_o_diff,
           q_norm, w_uq, kv_norm, w_ukv, w_o_mla, w_out, ffn_norm, w_ffn_gate, w_ffn_up,
           w_ffn_down, ple_norm, w_ple_gate, b_ple_gate, w_ple, final_norm):
    B, S, D = x.shape
    T = B * S
    assert D == D_MODEL and w_in.shape[0] == 1
    x2 = x.reshape(T, D)
    p2 = p[0].reshape(T, PLE_DIM)

    wi = w_in[0]
    o_cq = 3 * D
    o_ckv = o_cq + MLA_Q_LORA
    o_kr = o_ckv + MLA_KV_LORA
    o_g = o_kr + MLA_ROPE_DIM
    w_qkv = wi[:, :o_cq].astype(BF16)
    zeros = lambda n: jnp.zeros((D, n), F32)
    w_c = jnp.concatenate(
        [wi[:, o_cq:o_kr], zeros(MLA_NOPE_DIM), wi[:, o_kr:o_g],
         zeros(LANES - MLA_NOPE_DIM - MLA_ROPE_DIM)], axis=1).astype(BF16)
    w_g = wi[:, o_g:].astype(BF16)
    b_g = b_gate[0].reshape(1, 2 * D)

    wuq = w_uq[0].reshape(MLA_Q_LORA, MLA_HEADS, MLA_NOPE_DIM + MLA_ROPE_DIM)
    wuq = jnp.pad(wuq, ((0, 0), (0, 0), (0, LANES - MLA_NOPE_DIM - MLA_ROPE_DIM)))
    wuq = wuq.reshape(MLA_Q_LORA, MLA_HEADS * LANES).astype(BF16)
    wukv = w_ukv[0].reshape(MLA_KV_LORA, MLA_HEADS, MLA_NOPE_DIM + MLA_V_DIM)
    wk = jnp.pad(wukv[:, :, :MLA_NOPE_DIM], ((0, 0), (0, 0), (0, LANES - MLA_NOPE_DIM)))
    wv = wukv[:, :, MLA_NOPE_DIM:]
    wukv = jnp.concatenate([wk.reshape(MLA_KV_LORA, MLA_HEADS * LANES),
                            wv.reshape(MLA_KV_LORA, MLA_HEADS * MLA_V_DIM)], axis=1).astype(BF16)

    d_scale = DIFF_HEAD_DIM ** -0.5 * LOG2E
    m_scale = (MLA_NOPE_DIM + MLA_ROPE_DIM) ** -0.5 * LOG2E
    dq_tab = _rope_tables(S, DIFF_ROT_DIM, ROPE_THETA, 0, DIFF_HEAD_DIM, d_scale, 1.0)
    dk_tab = _rope_tables(S, DIFF_ROT_DIM, ROPE_THETA, 0, DIFF_HEAD_DIM, 1.0, 1.0)
    mq_tab = _rope_tables(S, MLA_ROPE_DIM, MLA_ROPE_THETA, MLA_NOPE_DIM, LANES, m_scale, 1.0)
    mq_tab = mq_tab * (jnp.arange(LANES) < MLA_NOPE_DIM + MLA_ROPE_DIM)
    mk_tab = _rope_tables(S, MLA_ROPE_DIM, MLA_ROPE_THETA, MLA_NOPE_DIM, LANES, 1.0, 0.0)

    row = lambda a: a.reshape(1, -1)

    tm = ROW_TILE_PROJ
    tiles_per_seq = S // tm
    tab_spec = pl.BlockSpec((3, tm, LANES), lambda i: (0, i % tiles_per_seq, 0))
    act = lambda n: pl.BlockSpec((tm, n), lambda i: (i, 0))
    dq, dk, dv, mq, mk, mv = pl.pallas_call(
        _proj_kernel,
        grid=(T // tm,),
        in_specs=[act(D), _const_spec((1, D)), _const_spec(w_qkv.shape), _const_spec(w_c.shape),
                  _const_spec((1, MLA_Q_LORA)), _const_spec(wuq.shape),
                  _const_spec((1, MLA_KV_LORA)), _const_spec(wukv.shape),
                  tab_spec, tab_spec, tab_spec, tab_spec],
        out_specs=[act(D), act(D), act(D), act(D), act(D), act(MLA_HEADS * MLA_V_DIM)],
        out_shape=[jax.ShapeDtypeStruct((T, D), BF16)] * 5
                  + [jax.ShapeDtypeStruct((T, MLA_HEADS * MLA_V_DIM), BF16)],
        compiler_params=_params("parallel"),
        name="proj",
    )(x2, row(attn_norm[0]), w_qkv, w_c, row(q_norm[0]), wuq, row(kv_norm[0]), wukv,
      dq_tab, dk_tab, mq_tab, mk_tab)

    head_blk = pl.BlockSpec((S, LANES), lambda b, h: (b, h))
    lam_spec = _const_spec((1, DIFF_HEAD_DIM))
    score_buf = pltpu.VMEM((SCORE_LOOKAHEAD + 1, Q_TILE, 2 * Q_TILE), F32)
    od = pl.pallas_call(
        functools.partial(_diff_attn_kernel, seq=S, tq=Q_TILE),
        grid=(B, DIFF_HEADS),
        in_specs=[lam_spec, lam_spec, lam_spec, lam_spec, _const_spec((1, DIFF_V_DIM)),
                  head_blk, head_blk, head_blk],
        out_specs=head_blk,
        out_shape=jax.ShapeDtypeStruct((T, DIFF_HEADS * DIFF_V_DIM), BF16),
        scratch_shapes=[pltpu.VMEM((DIFF_V_DIM + ONES_ROWS, S), BF16), score_buf],
        compiler_params=_params("parallel", "parallel", flags=ATTN_FLAGS),
        name="diff_attn",
    )(lam_q1, lam_k1, lam_q2, lam_k2, row(diff_subln[0]), dq, dk, dv)

    pair_blk = pl.BlockSpec((S, 2 * LANES), lambda b, g: (b, g))
    om = pl.pallas_call(
        functools.partial(_mla_attn_kernel, seq=S, tq=Q_TILE),
        grid=(B, MLA_HEADS // 2),
        in_specs=[pair_blk, pair_blk, head_blk],
        out_specs=head_blk,
        out_shape=jax.ShapeDtypeStruct((T, MLA_HEADS * MLA_V_DIM), BF16),
        scratch_shapes=[pltpu.VMEM((2 * MLA_V_DIM + ONES_ROWS, S), BF16), score_buf],
        compiler_params=_params("parallel", "parallel", flags=ATTN_FLAGS),
        name="mla_attn",
    )(mq, mk, mv)

    tm = ROW_TILE_MLP
    act = lambda n: pl.BlockSpec((tm, n), lambda i: (i, 0))
    weights = [w_g, b_g, w_o_diff[0].astype(BF16), w_o_mla[0].astype(BF16), w_out[0].astype(BF16),
               row(ffn_norm[0]), w_ffn_gate[0].astype(BF16), w_ffn_up[0].astype(BF16),
               w_ffn_down[0].astype(BF16), row(ple_norm[0]), w_ple_gate[0].astype(BF16),
               row(b_ple_gate[0]), w_ple[0].astype(BF16), row(final_norm)]
    out = pl.pallas_call(
        _mlp_kernel,
        grid=(T // tm,),
        in_specs=[act(D), act(D), act(MLA_HEADS * MLA_V_DIM), act(PLE_DIM), _const_spec((1, D))]
                 + [_const_spec(w.shape) for w in weights],
        out_specs=act(D),
        out_shape=jax.ShapeDtypeStruct((T, D), F32),
        compiler_params=_params("parallel"),
        name="mlp",
    )(x2, od, om, p2, row(attn_norm[0]), *weights)
    return out.reshape(B, S, D)
```

```python
import functools
import math

import jax
import jax.numpy as jnp
import numpy as np
from jax import lax
from jax.experimental import pallas as pl
from jax.experimental.pallas import tpu as pltpu

D_MODEL = 1024
CHUNK = 64
NORM_EPS = 1e-6
NEG_INF = -1e30

DIFF_HEAD_DIM = 64
DIFF_HEADS = 8
DIFF_V_DIM = 128
DIFF_ROT_DIM = 16
ROPE_THETA = 500000.0
LAM_INIT = 0.8 - 0.6 * math.exp(-0.3 * 0)

MLA_HEADS = 8
MLA_NOPE_DIM = 64
MLA_ROPE_DIM = 32
MLA_V_DIM = 64
MLA_Q_LORA = 384
MLA_KV_LORA = 256
MLA_ROPE_THETA = 10000.0

FFN_HIDDEN = 2816
PLE_DIM = 256

LANES = 128
ONES_ROWS = 16
LOG2E = math.log2(math.e)
VMEM_LIMIT_BYTES = 56 * 1024 * 1024

ROW_TILE_PROJ = 512
ROW_TILE_MLP = 512
MLP_ROW_GROUPS = 2
Q_TILE = 256
KEY_BLOCK = 256
SCORE_LOOKAHEAD = 6

BF16 = jnp.bfloat16
F32 = jnp.float32


def _rmsnorm(x, g):
    ms = jnp.mean(x * x, axis=-1, keepdims=True)
    return x * lax.rsqrt(ms + NORM_EPS) * g


def _dot(a, b):
    return jnp.dot(a, b, preferred_element_type=F32)


def _dot_nt(a, b):
    return lax.dot_general(a, b, (((1,), (1,)), ((), ())), preferred_element_type=F32)


def _sigmoid(x):
    return 1.0 / (1.0 + jnp.exp(-x))


def _rope_block(x, tab):
    return x * tab[0] + pltpu.roll(x, LANES // 2, 1) * tab[1]


def _proj_kernel(x_ref, an_ref, wqkv_ref, wc_ref, qn_ref, wuq_ref, kvn_ref, wukv_ref,
                 dq_tab, dk_tab, mq_tab, mk_tab,
                 dq_ref, dk_ref, dv_ref, mq_ref, mk_ref, mv_ref):
    h = _rmsnorm(x_ref[...], an_ref[...]).astype(BF16)

    c = _dot(h, wc_ref[...])
    cq = _rmsnorm(c[:, :MLA_Q_LORA], qn_ref[...]).astype(BF16)
    ckv = _rmsnorm(c[:, MLA_Q_LORA:MLA_Q_LORA + MLA_KV_LORA], kvn_ref[...]).astype(BF16)
    kr = _rope_block(c[:, MLA_Q_LORA + MLA_KV_LORA:], mk_tab)

    q = _dot(cq, wuq_ref[...])
    kv = _dot(ckv, wukv_ref[...])
    dq = _dot(h, wqkv_ref[:, 0:D_MODEL])
    for blk in range(MLA_HEADS):
        lo = blk * LANES
        mq_ref[:, lo:lo + LANES] = _rope_block(q[:, lo:lo + LANES], mq_tab).astype(BF16)
        mk_ref[:, lo:lo + LANES] = (kv[:, lo:lo + LANES] + kr).astype(BF16)
    mv_ref[...] = kv[:, MLA_HEADS * LANES:].astype(BF16)

    dk = _dot(h, wqkv_ref[:, D_MODEL:2 * D_MODEL])
    for blk in range(DIFF_HEADS):
        lo = blk * LANES
        dq_ref[:, lo:lo + LANES] = _rope_block(dq[:, lo:lo + LANES], dq_tab).astype(BF16)
    dv = _dot(h, wqkv_ref[:, 2 * D_MODEL:])
    for blk in range(DIFF_HEADS):
        lo = blk * LANES
        dk_ref[:, lo:lo + LANES] = _rope_block(dk[:, lo:lo + LANES], dk_tab).astype(BF16)
    dv_ref[...] = dv.astype(BF16)


def _pair_mask_t(tq):
    key = lax.broadcasted_iota(jnp.int32, (tq, 2 * tq), 0) // CHUNK
    qry = (lax.broadcasted_iota(jnp.int32, (tq, 2 * tq), 1) % tq) // CHUNK
    return key <= qry


def _block_diag_q(q, is_first):
    first = is_first(lax.broadcasted_iota(jnp.int32, q.shape, 1))
    zero = jnp.zeros((), q.dtype)
    return jnp.concatenate([jnp.where(first, q, zero), jnp.where(first, zero, q)], axis=0)


def _diff_first_map(lane):
    within = lane % (LANES // 2)
    half, rest = DIFF_ROT_DIM // 2, (DIFF_HEAD_DIM - DIFF_ROT_DIM) // 2
    return (within < half) | ((within >= 2 * half) & (within < 2 * half + rest))


def _mla_first_head(lane):
    return lane < LANES


def _attend_tiles(k_ref, q_ref, split, vt_ref, seq, tq, finish):
    n = seq // tq
    mask_t = _pair_mask_t(tq)
    chunks = [(t, k0, min(k0 + KEY_BLOCK, (t + 1) * tq))
              for t in range(n) for k0 in range(0, (t + 1) * tq, KEY_BLOCK)]
    q_bd = {}

    def scores(t, k0, k1):
        if t not in q_bd:
            q_bd[t] = _block_diag_q(q_ref[t * tq:(t + 1) * tq, :], split)
        s = _dot_nt(k_ref[k0:k1, :], q_bd[t])
        if k1 < (t + 1) * tq:
            return s
        diag = jnp.where(mask_t, s[k1 - k0 - tq:], NEG_INF)
        return diag if k1 - k0 == tq else jnp.concatenate([s[:k1 - k0 - tq], diag], axis=0)

    pending = [scores(*ch) for ch in chunks[:SCORE_LOOKAHEAD]]
    run_max = acc = None
    for c, (t, k0, k1) in enumerate(chunks):
        s = pending.pop(0)
        m = jnp.max(s, axis=0, keepdims=True)
        new_max = m if k0 == 0 else jnp.maximum(run_max, m)
        e = jnp.exp2((s - new_max).astype(BF16))
        part = _dot(vt_ref[:, k0:k1], e)
        if c + SCORE_LOOKAHEAD < len(chunks):
            pending.append(scores(*chunks[c + SCORE_LOOKAHEAD]))
        acc = part if k0 == 0 else acc * jnp.exp2(run_max - new_max) + part
        run_max = new_max
        if k1 == (t + 1) * tq:
            finish(t * tq, acc)


def _fill_vt(vt_ref, v_ref):
    width = v_ref.shape[1]
    vt_ref[0:width, :] = v_ref[...].T
    vt_ref[width:, :] = jnp.ones((vt_ref.shape[0] - width, vt_ref.shape[1]), BF16)


def _diff_attn_kernel(lq1_ref, lk1_ref, lq2_ref, lk2_ref, subln_ref, q_ref, k_ref, v_ref, o_ref,
                      vt_ref, *, seq, tq):
    lam = (jnp.exp(jnp.sum(lq1_ref[...] * lk1_ref[...], axis=-1, keepdims=True))
           - jnp.exp(jnp.sum(lq2_ref[...] * lk2_ref[...], axis=-1, keepdims=True))
           + LAM_INIT)
    dv = DIFF_V_DIM
    _fill_vt(vt_ref, v_ref)

    def finish(lo, a):
        o_t = (a[0:dv, 0:tq] * (1.0 / a[dv:dv + 1, 0:tq])
               - a[0:dv, tq:] * (lam / a[dv:dv + 1, tq:]))
        o = _rmsnorm(o_t.T, subln_ref[...]) * (1.0 - LAM_INIT)
        o_ref[lo:lo + tq, :] = o.astype(BF16)

    _attend_tiles(k_ref, q_ref, _diff_first_map, vt_ref, seq, tq, finish)


def _mla_attn_kernel(q_ref, k_ref, v_ref, o_ref, vt_ref, *, seq, tq):
    dv = MLA_V_DIM
    _fill_vt(vt_ref, v_ref)

    def finish(lo, a):
        o_t = jnp.concatenate(
            [a[0:dv, 0:tq] * (1.0 / a[2 * dv:2 * dv + 1, 0:tq]),
             a[dv:2 * dv, tq:] * (1.0 / a[2 * dv:2 * dv + 1, tq:])], axis=0)
        o_ref[lo:lo + tq, :] = o_t.T.astype(BF16)

    _attend_tiles(k_ref, q_ref, _mla_first_head, vt_ref, seq, tq, finish)


def _mlp_kernel(x_ref, od_ref, om_ref, p_ref, an_ref, wg_ref, bg_ref, wod_ref, wom_ref, wout_ref,
                fn_ref, wfg_ref, wfu_ref, wfd_ref, pn_ref, wpg_ref, bpg_ref, wple_ref, final_ref,
                o_ref):
    tm = x_ref.shape[0]
    groups = [slice(g * tm // MLP_ROW_GROUPS, (g + 1) * tm // MLP_ROW_GROUPS)
              for g in range(MLP_ROW_GROUPS)]

    def mix(r):
        x = x_ref[r, :]
        h = _rmsnorm(x, an_ref[...]).astype(BF16)
        gates = _sigmoid(_dot(h, wg_ref[...]) + bg_ref[...])
        out_a = _dot(od_ref[r, :], wod_ref[...])
        out_b = _dot(om_ref[r, :], wom_ref[...])
        merged = gates[:, :D_MODEL] * out_a + gates[:, D_MODEL:] * out_b
        return x + _dot(merged.astype(BF16), wout_ref[...])

    def ffn(x):
        h = _rmsnorm(x, fn_ref[...]).astype(BF16)
        gt = _dot(h, wfg_ref[...])
        up = _dot(h, wfu_ref[...])
        hid = (gt * _sigmoid(gt) * up).astype(BF16)
        return x + _dot(hid, wfd_ref[...])

    def embed(x, r):
        h = _rmsnorm(x, pn_ref[...]).astype(BF16)
        gate = _sigmoid(_dot(h, wpg_ref[...]) + bpg_ref[...])
        return x + _dot(p_ref[r, :].astype(BF16), wple_ref[...]) * gate

    xs = [mix(r) for r in groups]
    xs = [ffn(x) for x in xs]
    xs = [embed(x, r) for x, r in zip(xs, groups)]
    for x, r in zip(xs, groups):
        o_ref[r, :] = _rmsnorm(x, final_ref[...])


def _diff_lane_source():
    half, rest = DIFF_ROT_DIM // 2, (DIFF_HEAD_DIM - DIFF_ROT_DIM) // 2
    src = []
    for part in range(2):
        for m in range(2):
            src += [m * DIFF_HEAD_DIM + part * half + j for j in range(half)]
        for m in range(2):
            src += [m * DIFF_HEAD_DIM + DIFF_ROT_DIM + part * rest + j for j in range(rest)]
    return np.array(src)


def _mla_lane_sources():
    half = MLA_ROPE_DIM // 2
    first_nope = LANES // 2 - half
    nope = np.full(LANES, -1)
    rope = np.full(LANES, -1)
    rope[0:half] = np.arange(half)
    nope[half:LANES // 2] = np.arange(first_nope)
    rope[LANES // 2:LANES // 2 + half] = half + np.arange(half)
    rest = MLA_NOPE_DIM - first_nope
    nope[LANES // 2 + half:LANES // 2 + half + rest] = first_nope + np.arange(rest)
    return nope, rope


def _take_cols(w, idx):
    w = jnp.concatenate([w, jnp.zeros((w.shape[0], 1), w.dtype)], axis=1)
    return jnp.take(w, np.where(idx < 0, w.shape[1] - 1, idx), axis=1)


def _rope_tables(seq, rot_dim, theta, freq, passthrough, scale):
    pos = jnp.arange(seq, dtype=F32)
    inv_freq = theta ** (-(jnp.arange(0, rot_dim, 2, dtype=F32) / rot_dim))
    ang = pos[:, None] * inv_freq[None, :]
    cos_l = jnp.take(jnp.cos(ang), np.maximum(freq, 0), axis=1)
    sin_l = jnp.take(jnp.sin(ang), np.maximum(freq, 0), axis=1)
    sign = np.where(np.arange(LANES) < LANES // 2, -1.0, 1.0)
    c = jnp.where(freq >= 0, cos_l, passthrough.astype(np.float32))
    s = jnp.where(freq >= 0, sin_l * sign, 0.0)
    return jnp.stack([c, s]).astype(F32) * scale


def _const_spec(shape):
    nd = len(shape)
    return pl.BlockSpec(shape, lambda *_: (0,) * nd, pipeline_mode=pl.Buffered(1))


def _params(*sem, flags=None):
    return pltpu.CompilerParams(dimension_semantics=sem, vmem_limit_bytes=VMEM_LIMIT_BYTES,
                                flags=flags)


ATTN_FLAGS = None


def kernel(x, p, attn_norm, w_in, b_gate, lam_q1, lam_k1, lam_q2, lam_k2, diff_subln, w_o_diff,
           q_norm, w_uq, kv_norm, w_ukv, w_o_mla, w_out, ffn_norm, w_ffn_gate, w_ffn_up,
           w_ffn_down, ple_norm, w_ple_gate, b_ple_gate, w_ple, final_norm):
    B, S, D = x.shape
    T = B * S
    assert D == D_MODEL and w_in.shape[0] == 1
    x2 = x.reshape(T, D)
    p2 = p[0].reshape(T, PLE_DIM)

    wi = w_in[0]
    o_cq = 3 * D
    o_ckv = o_cq + MLA_Q_LORA
    o_kr = o_ckv + MLA_KV_LORA
    o_g = o_kr + MLA_ROPE_DIM
    heads = np.arange(DIFF_HEADS)[:, None]
    diff_src = _diff_lane_source()
    diff_cols = (heads * LANES + diff_src[None, :]).reshape(-1)
    w_qkv = jnp.concatenate([jnp.take(wi[:, :D], diff_cols, axis=1),
                             jnp.take(wi[:, D:2 * D], diff_cols, axis=1),
                             wi[:, 2 * D:o_cq]], axis=1).astype(BF16)
    nope_src, rope_src = _mla_lane_sources()
    w_c = jnp.concatenate([wi[:, o_cq:o_kr], _take_cols(wi[:, o_kr:o_g], rope_src)],
                          axis=1).astype(BF16)
    w_g = wi[:, o_g:].astype(BF16)
    b_g = b_gate[0].reshape(1, 2 * D)

    q_width = MLA_NOPE_DIM + MLA_ROPE_DIM
    q_src = np.where(nope_src >= 0, nope_src, np.where(rope_src >= 0, MLA_NOPE_DIM + rope_src, -1))
    q_cols = np.where(q_src[None, :] >= 0, heads * q_width + q_src[None, :], -1).reshape(-1)
    wuq = _take_cols(w_uq[0], q_cols).astype(BF16)
    kv_width = MLA_NOPE_DIM + MLA_V_DIM
    k_cols = np.where(nope_src[None, :] >= 0, heads * kv_width + nope_src[None, :], -1).reshape(-1)
    v_cols = (heads * kv_width + MLA_NOPE_DIM + np.arange(MLA_V_DIM)[None, :]).reshape(-1)
    wukv = _take_cols(w_ukv[0], np.concatenate([k_cols, v_cols])).astype(BF16)

    d_scale = DIFF_HEAD_DIM ** -0.5 * LOG2E
    m_scale = q_width ** -0.5 * LOG2E
    d_dim = diff_src % DIFF_HEAD_DIM
    d_freq = np.where(d_dim < DIFF_ROT_DIM, d_dim % (DIFF_ROT_DIM // 2), -1)
    ones = np.ones(LANES)
    dq_tab = _rope_tables(S, DIFF_ROT_DIM, ROPE_THETA, d_freq, ones, d_scale)
    dk_tab = _rope_tables(S, DIFF_ROT_DIM, ROPE_THETA, d_freq, ones, 1.0)
    m_freq = np.where(rope_src >= 0, rope_src % (MLA_ROPE_DIM // 2), -1)
    mq_tab = _rope_tables(S, MLA_ROPE_DIM, MLA_ROPE_THETA, m_freq, nope_src >= 0, m_scale)
    mk_tab = _rope_tables(S, MLA_ROPE_DIM, MLA_ROPE_THETA, m_freq, np.zeros(LANES), 1.0)

    row = lambda a: a.reshape(1, -1)

    tm = ROW_TILE_PROJ
    tiles_per_seq = S // tm
    tab_spec = pl.BlockSpec((2, tm, LANES), lambda i: (0, i % tiles_per_seq, 0))
    act = lambda n: pl.BlockSpec((tm, n), lambda i: (i, 0))
    dq, dk, dv, mq, mk, mv = pl.pallas_call(
        _proj_kernel,
        grid=(T // tm,),
        in_specs=[act(D), _const_spec((1, D)), _const_spec(w_qkv.shape), _const_spec(w_c.shape),
                  _const_spec((1, MLA_Q_LORA)), _const_spec(wuq.shape),
                  _const_spec((1, MLA_KV_LORA)), _const_spec(wukv.shape),
                  tab_spec, tab_spec, tab_spec, tab_spec],
        out_specs=[act(D), act(D), act(D), act(D), act(D), act(MLA_HEADS * MLA_V_DIM)],
        out_shape=[jax.ShapeDtypeStruct((T, D), BF16)] * 5
                  + [jax.ShapeDtypeStruct((T, MLA_HEADS * MLA_V_DIM), BF16)],
        compiler_params=_params("parallel"),
        name="proj",
    )(x2, row(attn_norm[0]), w_qkv, w_c, row(q_norm[0]), wuq, row(kv_norm[0]), wukv,
      dq_tab, dk_tab, mq_tab, mk_tab)

    head_blk = pl.BlockSpec((S, LANES), lambda b, h: (b, h))
    lam_spec = _const_spec((1, DIFF_HEAD_DIM))
    od = pl.pallas_call(
        functools.partial(_diff_attn_kernel, seq=S, tq=Q_TILE),
        grid=(B, DIFF_HEADS),
        in_specs=[lam_spec, lam_spec, lam_spec, lam_spec, _const_spec((1, DIFF_V_DIM)),
                  head_blk, head_blk, head_blk],
        out_specs=head_blk,
        out_shape=jax.ShapeDtypeStruct((T, DIFF_HEADS * DIFF_V_DIM), BF16),
        scratch_shapes=[pltpu.VMEM((DIFF_V_DIM + ONES_ROWS, S), BF16)],
        compiler_params=_params("parallel", "parallel", flags=ATTN_FLAGS),
        name="diff_attn",
    )(lam_q1, lam_k1, lam_q2, lam_k2, row(diff_subln[0]), dq, dk, dv)

    pair_blk = pl.BlockSpec((S, 2 * LANES), lambda b, g: (b, g))
    om = pl.pallas_call(
        functools.partial(_mla_attn_kernel, seq=S, tq=Q_TILE),
        grid=(B, MLA_HEADS // 2),
        in_specs=[pair_blk, pair_blk, head_blk],
        out_specs=head_blk,
        out_shape=jax.ShapeDtypeStruct((T, MLA_HEADS * MLA_V_DIM), BF16),
        scratch_shapes=[pltpu.VMEM((2 * MLA_V_DIM + ONES_ROWS, S), BF16)],
        compiler_params=_params("parallel", "parallel", flags=ATTN_FLAGS),
        name="mla_attn",
    )(mq, mk, mv)

    tm = ROW_TILE_MLP
    act = lambda n: pl.BlockSpec((tm, n), lambda i: (i, 0))
    weights = [w_g, b_g, w_o_diff[0].astype(BF16), w_o_mla[0].astype(BF16), w_out[0].astype(BF16),
               row(ffn_norm[0]), w_ffn_gate[0].astype(BF16), w_ffn_up[0].astype(BF16),
               w_ffn_down[0].astype(BF16), row(ple_norm[0]), w_ple_gate[0].astype(BF16),
               row(b_ple_gate[0]), w_ple[0].astype(BF16), row(final_norm)]
    out = pl.pallas_call(
        _mlp_kernel,
        grid=(T // tm,),
        in_specs=[act(D), act(D), act(MLA_HEADS * MLA_V_DIM), act(PLE_DIM), _const_spec((1, D))]
                 + [_const_spec(w.shape) for w in weights],
        out_specs=act(D),
        out_shape=jax.ShapeDtypeStruct((T, D), F32),
        compiler_params=_params("parallel"),
        name="mlp",
    )(x2, od, om, p2, row(attn_norm[0]), *weights)
    return out.reshape(B, S, D)
```

```python
import functools
import math

import jax
import jax.numpy as jnp
import numpy as np
from jax import lax
from jax.experimental import pallas as pl
from jax.experimental.pallas import tpu as pltpu

D_MODEL = 1024
CHUNK = 64
NORM_EPS = 1e-6
NEG_INF = -1e30

DIFF_HEAD_DIM = 64
DIFF_HEADS = 8
DIFF_V_DIM = 128
DIFF_ROT_DIM = 16
ROPE_THETA = 500000.0
LAM_INIT = 0.8 - 0.6 * math.exp(-0.3 * 0)

MLA_HEADS = 8
MLA_NOPE_DIM = 64
MLA_ROPE_DIM = 32
MLA_V_DIM = 64
MLA_Q_LORA = 384
MLA_KV_LORA = 256
MLA_ROPE_THETA = 10000.0

FFN_HIDDEN = 2816
PLE_DIM = 256

LANES = 128
ONES_ROWS = 16
LOG2E = math.log2(math.e)
VMEM_LIMIT_BYTES = 56 * 1024 * 1024

ROW_TILE_PROJ = 512
ROW_TILE_MLP = 512
MLP_ROW_GROUPS = 2
Q_TILE = 256
KEY_BLOCK = 256
SCORE_LOOKAHEAD = 6

BF16 = jnp.bfloat16
F32 = jnp.float32


def _rmsnorm(x, g):
    ms = jnp.mean(x * x, axis=-1, keepdims=True)
    return x * lax.rsqrt(ms + NORM_EPS) * g


def _dot(a, b):
    return jnp.dot(a, b, preferred_element_type=F32)


def _dot_nt(a, b):
    return lax.dot_general(a, b, (((1,), (1,)), ((), ())), preferred_element_type=F32)


def _sigmoid(x):
    return 1.0 / (1.0 + jnp.exp(-x))


def _rope_block(x, tab):
    return x * tab[0] + pltpu.roll(x, LANES // 2, 1) * tab[1]


def _proj_kernel(x_ref, an_ref, wqkv_ref, wc_ref, qn_ref, wuq_ref, kvn_ref, wukv_ref,
                 dq_tab, dk_tab, mq_tab, mk_tab,
                 dq_ref, dk_ref, dv_ref, mq_ref, mk_ref, mv_ref):
    h = _rmsnorm(x_ref[...], an_ref[...]).astype(BF16)

    c = _dot(h, wc_ref[...])
    cq = _rmsnorm(c[:, :MLA_Q_LORA], qn_ref[...]).astype(BF16)
    ckv = _rmsnorm(c[:, MLA_Q_LORA:MLA_Q_LORA + MLA_KV_LORA], kvn_ref[...]).astype(BF16)
    kr = _rope_block(c[:, MLA_Q_LORA + MLA_KV_LORA:], mk_tab)

    q = _dot(cq, wuq_ref[...])
    kv = _dot(ckv, wukv_ref[...])
    dq = _dot(h, wqkv_ref[:, 0:D_MODEL])
    for blk in range(MLA_HEADS):
        lo = blk * LANES
        pair, side = blk // 2, slice((blk % 2) * LANES, (blk % 2 + 1) * LANES)
        mq_ref[pair, :, side] = _rope_block(q[:, lo:lo + LANES], mq_tab).astype(BF16)
        mk_ref[pair, :, side] = (kv[:, lo:lo + LANES] + kr).astype(BF16)
    for pair in range(MLA_HEADS // 2):
        lo = MLA_HEADS * LANES + pair * LANES
        mv_ref[pair] = kv[:, lo:lo + LANES].astype(BF16)

    dk = _dot(h, wqkv_ref[:, D_MODEL:2 * D_MODEL])
    for blk in range(DIFF_HEADS):
        lo = blk * LANES
        dq_ref[blk] = _rope_block(dq[:, lo:lo + LANES], dq_tab).astype(BF16)
    dv = _dot(h, wqkv_ref[:, 2 * D_MODEL:])
    for blk in range(DIFF_HEADS):
        lo = blk * LANES
        dk_ref[blk] = _rope_block(dk[:, lo:lo + LANES], dk_tab).astype(BF16)
        dv_ref[blk] = dv[:, lo:lo + LANES].astype(BF16)


def _pair_mask_t(tq):
    key = lax.broadcasted_iota(jnp.int32, (tq, 2 * tq), 0) // CHUNK
    qry = (lax.broadcasted_iota(jnp.int32, (tq, 2 * tq), 1) % tq) // CHUNK
    return key <= qry


def _block_diag_q(q, is_first):
    first = is_first(lax.broadcasted_iota(jnp.int32, q.shape, 1))
    zero = jnp.zeros((), q.dtype)
    return jnp.concatenate([jnp.where(first, q, zero), jnp.where(first, zero, q)], axis=0)


def _diff_first_map(lane):
    within = lane % (LANES // 2)
    half, rest = DIFF_ROT_DIM // 2, (DIFF_HEAD_DIM - DIFF_ROT_DIM) // 2
    return (within < half) | ((within >= 2 * half) & (within < 2 * half + rest))


def _mla_first_head(lane):
    return lane < LANES


def _attend_tiles(k_ref, q_ref, split, vt_ref, seq, tq, finish):
    n = seq // tq
    mask_t = _pair_mask_t(tq)
    chunks = [(t, k0, min(k0 + KEY_BLOCK, (t + 1) * tq))
              for t in range(n) for k0 in range(0, (t + 1) * tq, KEY_BLOCK)]
    q_bd = {}

    def scores(t, k0, k1):
        if t not in q_bd:
            q_bd[t] = _block_diag_q(q_ref[t * tq:(t + 1) * tq, :], split)
        s = _dot_nt(k_ref[k0:k1, :], q_bd[t])
        if k1 < (t + 1) * tq:
            return s
        diag = jnp.where(mask_t, s[k1 - k0 - tq:], NEG_INF)
        return diag if k1 - k0 == tq else jnp.concatenate([s[:k1 - k0 - tq], diag], axis=0)

    pending = [scores(*ch) for ch in chunks[:SCORE_LOOKAHEAD]]
    run_max = acc = None
    for c, (t, k0, k1) in enumerate(chunks):
        s = pending.pop(0)
        m = jnp.max(s, axis=0, keepdims=True)
        new_max = m if k0 == 0 else jnp.maximum(run_max, m)
        e = jnp.exp2((s - new_max).astype(BF16))
        part = _dot(vt_ref[:, k0:k1], e)
        if c + SCORE_LOOKAHEAD < len(chunks):
            pending.append(scores(*chunks[c + SCORE_LOOKAHEAD]))
        acc = part if k0 == 0 else acc * jnp.exp2(run_max - new_max) + part
        run_max = new_max
        if k1 == (t + 1) * tq:
            finish(t * tq, acc)


def _fill_vt(vt_ref, v_ref):
    width = v_ref.shape[1]
    vt_ref[0:width, :] = v_ref[...].T
    vt_ref[width:, :] = jnp.ones((vt_ref.shape[0] - width, vt_ref.shape[1]), BF16)


def _diff_attn_kernel(lq1_ref, lk1_ref, lq2_ref, lk2_ref, subln_ref, q_ref, k_ref, v_ref, o_ref,
                      vt_ref, *, seq, tq):
    lam = (jnp.exp(jnp.sum(lq1_ref[...] * lk1_ref[...], axis=-1, keepdims=True))
           - jnp.exp(jnp.sum(lq2_ref[...] * lk2_ref[...], axis=-1, keepdims=True))
           + LAM_INIT)
    dv = DIFF_V_DIM
    _fill_vt(vt_ref, v_ref)

    def finish(lo, a):
        o_t = (a[0:dv, 0:tq] * (1.0 / a[dv:dv + 1, 0:tq])
               - a[0:dv, tq:] * (lam / a[dv:dv + 1, tq:]))
        o = _rmsnorm(o_t.T, subln_ref[...]) * (1.0 - LAM_INIT)
        o_ref[lo:lo + tq, :] = o.astype(BF16)

    _attend_tiles(k_ref, q_ref, _diff_first_map, vt_ref, seq, tq, finish)


def _mla_attn_kernel(q_ref, k_ref, v_ref, o_ref, vt_ref, *, seq, tq):
    dv = MLA_V_DIM
    _fill_vt(vt_ref, v_ref)

    def finish(lo, a):
        o_t = jnp.concatenate(
            [a[0:dv, 0:tq] * (1.0 / a[2 * dv:2 * dv + 1, 0:tq]),
             a[dv:2 * dv, tq:] * (1.0 / a[2 * dv:2 * dv + 1, tq:])], axis=0)
        o_ref[lo:lo + tq, :] = o_t.T.astype(BF16)

    _attend_tiles(k_ref, q_ref, _mla_first_head, vt_ref, seq, tq, finish)


def _mlp_kernel(x_ref, od_ref, om_ref, p_ref, an_ref, wg_ref, bg_ref, wod_ref, wom_ref, wout_ref,
                fn_ref, wfg_ref, wfu_ref, wfd_ref, pn_ref, wpg_ref, bpg_ref, wple_ref, final_ref,
                o_ref):
    tm = x_ref.shape[0]
    groups = [slice(g * tm // MLP_ROW_GROUPS, (g + 1) * tm // MLP_ROW_GROUPS)
              for g in range(MLP_ROW_GROUPS)]

    def mix(r):
        x = x_ref[r, :]
        h = _rmsnorm(x, an_ref[...]).astype(BF16)
        gates = _sigmoid(_dot(h, wg_ref[...]) + bg_ref[...])
        od = jnp.concatenate([od_ref[h, r, :] for h in range(od_ref.shape[0])], axis=1)
        om = jnp.concatenate([om_ref[g, r, :] for g in range(om_ref.shape[0])], axis=1)
        out_a = _dot(od, wod_ref[...])
        out_b = _dot(om, wom_ref[...])
        merged = gates[:, :D_MODEL] * out_a + gates[:, D_MODEL:] * out_b
        return x + _dot(merged.astype(BF16), wout_ref[...])

    def ffn(x):
        h = _rmsnorm(x, fn_ref[...]).astype(BF16)
        gt = _dot(h, wfg_ref[...])
        up = _dot(h, wfu_ref[...])
        hid = (gt * _sigmoid(gt) * up).astype(BF16)
        return x + _dot(hid, wfd_ref[...])

    def embed(x, r):
        h = _rmsnorm(x, pn_ref[...]).astype(BF16)
        gate = _sigmoid(_dot(h, wpg_ref[...]) + bpg_ref[...])
        return x + _dot(p_ref[r, :].astype(BF16), wple_ref[...]) * gate

    xs = [mix(r) for r in groups]
    xs = [ffn(x) for x in xs]
    xs = [embed(x, r) for x, r in zip(xs, groups)]
    for x, r in zip(xs, groups):
        o_ref[r, :] = _rmsnorm(x, final_ref[...])


def _diff_lane_source():
    half, rest = DIFF_ROT_DIM // 2, (DIFF_HEAD_DIM - DIFF_ROT_DIM) // 2
    src = []
    for part in range(2):
        for m in range(2):
            src += [m * DIFF_HEAD_DIM + part * half + j for j in range(half)]
        for m in range(2):
            src += [m * DIFF_HEAD_DIM + DIFF_ROT_DIM + part * rest + j for j in range(rest)]
    return np.array(src)


def _mla_lane_sources():
    half = MLA_ROPE_DIM // 2
    first_nope = LANES // 2 - half
    nope = np.full(LANES, -1)
    rope = np.full(LANES, -1)
    rope[0:half] = np.arange(half)
    nope[half:LANES // 2] = np.arange(first_nope)
    rope[LANES // 2:LANES // 2 + half] = half + np.arange(half)
    rest = MLA_NOPE_DIM - first_nope
    nope[LANES // 2 + half:LANES // 2 + half + rest] = first_nope + np.arange(rest)
    return nope, rope


def _take_cols(w, idx):
    idx = np.asarray(idx)
    same_run = ((np.diff(idx) == 1) & (idx[:-1] >= 0)) | ((idx[1:] < 0) & (idx[:-1] < 0))
    cuts = np.flatnonzero(~same_run) + 1
    pieces = []
    for run in np.split(idx, cuts):
        pieces.append(jnp.zeros((w.shape[0], len(run)), w.dtype) if run[0] < 0
                      else w[:, run[0]:run[0] + len(run)])
    return jnp.concatenate(pieces, axis=1)


def _rope_tables(seq, rot_dim, theta, freq, passthrough, scale):
    pos = jnp.arange(seq, dtype=F32)
    inv_freq = theta ** (-(jnp.arange(0, rot_dim, 2, dtype=F32) / rot_dim))
    ang = pos[:, None] * inv_freq[None, :]
    cos_l = jnp.take(jnp.cos(ang), np.maximum(freq, 0), axis=1)
    sin_l = jnp.take(jnp.sin(ang), np.maximum(freq, 0), axis=1)
    sign = np.where(np.arange(LANES) < LANES // 2, -1.0, 1.0)
    c = jnp.where(freq >= 0, cos_l, passthrough.astype(np.float32))
    s = jnp.where(freq >= 0, sin_l * sign, 0.0)
    return jnp.stack([c, s]).astype(F32) * scale


def _const_spec(shape):
    nd = len(shape)
    return pl.BlockSpec(shape, lambda *_: (0,) * nd, pipeline_mode=pl.Buffered(1))


def _params(*sem, flags=None):
    return pltpu.CompilerParams(dimension_semantics=sem, vmem_limit_bytes=VMEM_LIMIT_BYTES,
                                flags=flags)


ATTN_FLAGS = None


def kernel(x, p, attn_norm, w_in, b_gate, lam_q1, lam_k1, lam_q2, lam_k2, diff_subln, w_o_diff,
           q_norm, w_uq, kv_norm, w_ukv, w_o_mla, w_out, ffn_norm, w_ffn_gate, w_ffn_up,
           w_ffn_down, ple_norm, w_ple_gate, b_ple_gate, w_ple, final_norm):
    B, S, D = x.shape
    T = B * S
    assert D == D_MODEL and w_in.shape[0] == 1
    x2 = x.reshape(T, D)
    p2 = p[0].reshape(T, PLE_DIM)

    wi = w_in[0]
    o_cq = 3 * D
    o_ckv = o_cq + MLA_Q_LORA
    o_kr = o_ckv + MLA_KV_LORA
    o_g = o_kr + MLA_ROPE_DIM
    heads = np.arange(DIFF_HEADS)[:, None]
    diff_src = _diff_lane_source()
    diff_cols = (heads * LANES + diff_src[None, :]).reshape(-1)
    w_qkv = jnp.concatenate([_take_cols(wi[:, :D], diff_cols), _take_cols(wi[:, D:2 * D], diff_cols),
                             wi[:, 2 * D:o_cq]], axis=1).astype(BF16)
    nope_src, rope_src = _mla_lane_sources()
    w_c = jnp.concatenate([wi[:, o_cq:o_kr], _take_cols(wi[:, o_kr:o_g], rope_src)],
                          axis=1).astype(BF16)
    w_g = wi[:, o_g:].astype(BF16)
    b_g = b_gate[0].reshape(1, 2 * D)

    q_width = MLA_NOPE_DIM + MLA_ROPE_DIM
    q_src = np.where(nope_src >= 0, nope_src, np.where(rope_src >= 0, MLA_NOPE_DIM + rope_src, -1))
    q_cols = np.where(q_src[None, :] >= 0, heads * q_width + q_src[None, :], -1).reshape(-1)
    wuq = _take_cols(w_uq[0], q_cols).astype(BF16)
    kv_width = MLA_NOPE_DIM + MLA_V_DIM
    k_cols = np.where(nope_src[None, :] >= 0, heads * kv_width + nope_src[None, :], -1).reshape(-1)
    v_cols = (heads * kv_width + MLA_NOPE_DIM + np.arange(MLA_V_DIM)[None, :]).reshape(-1)
    wukv = _take_cols(w_ukv[0], np.concatenate([k_cols, v_cols])).astype(BF16)

    d_scale = DIFF_HEAD_DIM ** -0.5 * LOG2E
    m_scale = q_width ** -0.5 * LOG2E
    d_dim = diff_src % DIFF_HEAD_DIM
    d_freq = np.where(d_dim < DIFF_ROT_DIM, d_dim % (DIFF_ROT_DIM // 2), -1)
    ones = np.ones(LANES)
    dq_tab = _rope_tables(S, DIFF_ROT_DIM, ROPE_THETA, d_freq, ones, d_scale)
    dk_tab = _rope_tables(S, DIFF_ROT_DIM, ROPE_THETA, d_freq, ones, 1.0)
    m_freq = np.where(rope_src >= 0, rope_src % (MLA_ROPE_DIM // 2), -1)
    mq_tab = _rope_tables(S, MLA_ROPE_DIM, MLA_ROPE_THETA, m_freq, nope_src >= 0, m_scale)
    mk_tab = _rope_tables(S, MLA_ROPE_DIM, MLA_ROPE_THETA, m_freq, np.zeros(LANES), 1.0)

    row = lambda a: a.reshape(1, -1)

    tm = ROW_TILE_PROJ
    tiles_per_seq = S // tm
    tab_spec = pl.BlockSpec((2, tm, LANES), lambda i: (0, i % tiles_per_seq, 0))
    act = lambda n: pl.BlockSpec((tm, n), lambda i: (i, 0))
    heads_out = lambda n, w: pl.BlockSpec((n, tm, w), lambda i: (0, i, 0))
    pairs = MLA_HEADS // 2
    dq, dk, dv, mq, mk, mv = pl.pallas_call(
        _proj_kernel,
        grid=(T // tm,),
        in_specs=[act(D), _const_spec((1, D)), _const_spec(w_qkv.shape), _const_spec(w_c.shape),
                  _const_spec((1, MLA_Q_LORA)), _const_spec(wuq.shape),
                  _const_spec((1, MLA_KV_LORA)), _const_spec(wukv.shape),
                  tab_spec, tab_spec, tab_spec, tab_spec],
        out_specs=[heads_out(DIFF_HEADS, LANES)] * 3 + [heads_out(pairs, 2 * LANES)] * 2
                  + [heads_out(pairs, LANES)],
        out_shape=[jax.ShapeDtypeStruct((DIFF_HEADS, T, LANES), BF16)] * 3
                  + [jax.ShapeDtypeStruct((pairs, T, 2 * LANES), BF16)] * 2
                  + [jax.ShapeDtypeStruct((pairs, T, LANES), BF16)],
        compiler_params=_params("parallel"),
        name="proj",
    )(x2, row(attn_norm[0]), w_qkv, w_c, row(q_norm[0]), wuq, row(kv_norm[0]), wukv,
      dq_tab, dk_tab, mq_tab, mk_tab)

    head_blk = pl.BlockSpec((None, S, LANES), lambda b, h: (h, b, 0))
    lam_spec = _const_spec((1, DIFF_HEAD_DIM))
    od = pl.pallas_call(
        functools.partial(_diff_attn_kernel, seq=S, tq=Q_TILE),
        grid=(B, DIFF_HEADS),
        in_specs=[lam_spec, lam_spec, lam_spec, lam_spec, _const_spec((1, DIFF_V_DIM)),
                  head_blk, head_blk, head_blk],
        out_specs=head_blk,
        out_shape=jax.ShapeDtypeStruct((DIFF_HEADS, T, DIFF_V_DIM), BF16),
        scratch_shapes=[pltpu.VMEM((DIFF_V_DIM + ONES_ROWS, S), BF16)],
        compiler_params=_params("parallel", "parallel", flags=ATTN_FLAGS),
        name="diff_attn",
    )(lam_q1, lam_k1, lam_q2, lam_k2, row(diff_subln[0]), dq, dk, dv)

    pair_blk = pl.BlockSpec((None, S, 2 * LANES), lambda b, g: (g, b, 0))
    om = pl.pallas_call(
        functools.partial(_mla_attn_kernel, seq=S, tq=Q_TILE),
        grid=(B, MLA_HEADS // 2),
        in_specs=[pair_blk, pair_blk, head_blk],
        out_specs=head_blk,
        out_shape=jax.ShapeDtypeStruct((pairs, T, 2 * MLA_V_DIM), BF16),
        scratch_shapes=[pltpu.VMEM((2 * MLA_V_DIM + ONES_ROWS, S), BF16)],
        compiler_params=_params("parallel", "parallel", flags=ATTN_FLAGS),
        name="mla_attn",
    )(mq, mk, mv)

    tm = ROW_TILE_MLP
    act = lambda n: pl.BlockSpec((tm, n), lambda i: (i, 0))
    weights = [w_g, b_g, w_o_diff[0].astype(BF16), w_o_mla[0].astype(BF16), w_out[0].astype(BF16),
               row(ffn_norm[0]), w_ffn_gate[0].astype(BF16), w_ffn_up[0].astype(BF16),
               w_ffn_down[0].astype(BF16), row(ple_norm[0]), w_ple_gate[0].astype(BF16),
               row(b_ple_gate[0]), w_ple[0].astype(BF16), row(final_norm)]
    out = pl.pallas_call(
        _mlp_kernel,
        grid=(T // tm,),
        in_specs=[act(D), heads_out(DIFF_HEADS, DIFF_V_DIM), heads_out(pairs, 2 * MLA_V_DIM),
                  act(PLE_DIM), _const_spec((1, D))]
                 + [_const_spec(w.shape) for w in weights],
        out_specs=act(D),
        out_shape=jax.ShapeDtypeStruct((T, D), F32),
        compiler_params=_params("parallel"),
        name="mlp",
    )(x2, od, om, p2, row(attn_norm[0]), *weights)
    return out.reshape(B, S, D)
```

```python
import functools
import math

import jax
import jax.numpy as jnp
import numpy as np
from jax import lax
from jax.experimental import pallas as pl
from jax.experimental.pallas import tpu as pltpu

D_MODEL = 1024
CHUNK = 64
NORM_EPS = 1e-6
NEG_INF = -1e30

DIFF_HEAD_DIM = 64
DIFF_HEADS = 8
DIFF_V_DIM = 128
DIFF_ROT_DIM = 16
ROPE_THETA = 500000.0
LAM_INIT = 0.8 - 0.6 * math.exp(-0.3 * 0)

MLA_HEADS = 8
MLA_NOPE_DIM = 64
MLA_ROPE_DIM = 32
MLA_V_DIM = 64
MLA_Q_LORA = 384
MLA_KV_LORA = 256
MLA_ROPE_THETA = 10000.0

FFN_HIDDEN = 2816
PLE_DIM = 256

LANES = 128
ONES_ROWS = 16
LOG2E = math.log2(math.e)
VMEM_LIMIT_BYTES = 56 * 1024 * 1024

ROW_TILE_PROJ = 512
ROW_TILE_MLP = 512
MLP_ROW_GROUPS = 2
Q_TILE = 256
KEY_BLOCK = 256
SCORE_LOOKAHEAD = 6
ATTN_UNITS = 2

BF16 = jnp.bfloat16
F32 = jnp.float32


def _rmsnorm(x, g):
    ms = jnp.mean(x * x, axis=-1, keepdims=True)
    return x * lax.rsqrt(ms + NORM_EPS) * g


def _dot(a, b):
    return jnp.dot(a, b, preferred_element_type=F32)


def _dot_nt(a, b):
    return lax.dot_general(a, b, (((1,), (1,)), ((), ())), preferred_element_type=F32)


def _sigmoid(x):
    return 1.0 / (1.0 + jnp.exp(-x))


def _rope_block(x, tab):
    return x * tab[0] + pltpu.roll(x, LANES // 2, 1) * tab[1]


def _proj_kernel(x_ref, an_ref, wqkv_ref, wc_ref, qn_ref, wuq_ref, kvn_ref, wukv_ref,
                 dq_tab, dk_tab, mq_tab, mk_tab,
                 dq_ref, dk_ref, dv_ref, mq_ref, mk_ref, mv_ref):
    h = _rmsnorm(x_ref[...], an_ref[...]).astype(BF16)

    c = _dot(h, wc_ref[...])
    cq = _rmsnorm(c[:, :MLA_Q_LORA], qn_ref[...]).astype(BF16)
    ckv = _rmsnorm(c[:, MLA_Q_LORA:MLA_Q_LORA + MLA_KV_LORA], kvn_ref[...]).astype(BF16)
    kr = _rope_block(c[:, MLA_Q_LORA + MLA_KV_LORA:], mk_tab)

    q = _dot(cq, wuq_ref[...])
    kv = _dot(ckv, wukv_ref[...])
    dq = _dot(h, wqkv_ref[:, 0:D_MODEL])
    for blk in range(MLA_HEADS):
        lo = blk * LANES
        mq_ref[:, lo:lo + LANES] = _rope_block(q[:, lo:lo + LANES], mq_tab).astype(BF16)
        mk_ref[:, lo:lo + LANES] = (kv[:, lo:lo + LANES] + kr).astype(BF16)
    mv_ref[...] = kv[:, MLA_HEADS * LANES:].astype(BF16)

    dk = _dot(h, wqkv_ref[:, D_MODEL:2 * D_MODEL])
    for blk in range(DIFF_HEADS):
        lo = blk * LANES
        dq_ref[:, lo:lo + LANES] = _rope_block(dq[:, lo:lo + LANES], dq_tab).astype(BF16)
    dv = _dot(h, wqkv_ref[:, 2 * D_MODEL:])
    for blk in range(DIFF_HEADS):
        lo = blk * LANES
        dk_ref[:, lo:lo + LANES] = _rope_block(dk[:, lo:lo + LANES], dk_tab).astype(BF16)
    dv_ref[...] = dv.astype(BF16)


def _pair_mask_t(tq):
    key = lax.broadcasted_iota(jnp.int32, (tq, 2 * tq), 0) // CHUNK
    qry = (lax.broadcasted_iota(jnp.int32, (tq, 2 * tq), 1) % tq) // CHUNK
    return key <= qry


def _block_diag_q(q, is_first):
    first = is_first(lax.broadcasted_iota(jnp.int32, q.shape, 1))
    zero = jnp.zeros((), q.dtype)
    return jnp.concatenate([jnp.where(first, q, zero), jnp.where(first, zero, q)], axis=0)


def _diff_first_map(lane):
    within = lane % (LANES // 2)
    half, rest = DIFF_ROT_DIM // 2, (DIFF_HEAD_DIM - DIFF_ROT_DIM) // 2
    return (within < half) | ((within >= 2 * half) & (within < 2 * half + rest))


def _mla_first_head(lane):
    return lane < LANES


def _attend_tiles(units, is_first, seq, tq):
    n = seq // tq
    mask_t = _pair_mask_t(tq)
    chunks = [(u, t, k0, min(k0 + KEY_BLOCK, (t + 1) * tq))
              for u in range(len(units)) for t in range(n)
              for k0 in range(0, (t + 1) * tq, KEY_BLOCK)]
    q_bd = {}

    def scores(u, t, k0, k1):
        k, q = units[u][0], units[u][1]
        if (u, t) not in q_bd:
            q_bd[u, t] = _block_diag_q(q(slice(t * tq, (t + 1) * tq)), is_first)
        s = _dot_nt(k(slice(k0, k1)), q_bd[u, t])
        if k1 < (t + 1) * tq:
            return s
        diag = jnp.where(mask_t, s[k1 - k0 - tq:], NEG_INF)
        return diag if k1 - k0 == tq else jnp.concatenate([s[:k1 - k0 - tq], diag], axis=0)

    pending = [scores(*ch) for ch in chunks[:SCORE_LOOKAHEAD]]
    run_max = acc = None
    for c, (u, t, k0, k1) in enumerate(chunks):
        s = pending.pop(0)
        m = jnp.max(s, axis=0, keepdims=True)
        new_max = m if k0 == 0 else jnp.maximum(run_max, m)
        e = jnp.exp2((s - new_max).astype(BF16))
        part = _dot(units[u][2][:, k0:k1], e)
        if c + SCORE_LOOKAHEAD < len(chunks):
            pending.append(scores(*chunks[c + SCORE_LOOKAHEAD]))
        acc = part if k0 == 0 else acc * jnp.exp2(run_max - new_max) + part
        run_max = new_max
        if k1 == (t + 1) * tq:
            units[u][3](t * tq, acc)


def _fill_vt(vt_ref, v):
    width = v.shape[1]
    vt_ref[0:width, :] = v.T
    vt_ref[width:, :] = jnp.ones((vt_ref.shape[0] - width, vt_ref.shape[1]), BF16)


def _lane_block(ref, u, width):
    return lambda rows: ref[rows, u * width:(u + 1) * width]


def _diff_attn_kernel(lq1_ref, lk1_ref, lq2_ref, lk2_ref, subln_ref, q_ref, k_ref, v_ref, o_ref,
                      vt_ref, *, seq, tq):
    lam = (jnp.exp(jnp.sum(lq1_ref[...] * lk1_ref[...], axis=-1, keepdims=True))
           - jnp.exp(jnp.sum(lq2_ref[...] * lk2_ref[...], axis=-1, keepdims=True))
           + LAM_INIT)
    dv = DIFF_V_DIM

    def finisher(u):
        def finish(lo, a):
            o_t = (a[0:dv, 0:tq] * (1.0 / a[dv:dv + 1, 0:tq])
                   - a[0:dv, tq:] * (lam / a[dv:dv + 1, tq:]))
            o = _rmsnorm(o_t.T, subln_ref[...]) * (1.0 - LAM_INIT)
            o_ref[lo:lo + tq, u * dv:(u + 1) * dv] = o.astype(BF16)
        return finish

    units = []
    for u in range(vt_ref.shape[0]):
        _fill_vt(vt_ref.at[u], v_ref[:, u * dv:(u + 1) * dv])
        units.append((_lane_block(k_ref, u, LANES), _lane_block(q_ref, u, LANES), vt_ref.at[u],
                      finisher(u)))
    _attend_tiles(units, _diff_first_map, seq, tq)


def _mla_attn_kernel(q_ref, k_ref, v_ref, o_ref, vt_ref, *, seq, tq):
    dv = MLA_V_DIM

    def finisher(u):
        def finish(lo, a):
            o_t = jnp.concatenate(
                [a[0:dv, 0:tq] * (1.0 / a[2 * dv:2 * dv + 1, 0:tq]),
                 a[dv:2 * dv, tq:] * (1.0 / a[2 * dv:2 * dv + 1, tq:])], axis=0)
            o_ref[lo:lo + tq, u * LANES:(u + 1) * LANES] = o_t.T.astype(BF16)
        return finish

    units = []
    for u in range(vt_ref.shape[0]):
        _fill_vt(vt_ref.at[u], v_ref[:, u * LANES:(u + 1) * LANES])
        units.append((_lane_block(k_ref, u, 2 * LANES), _lane_block(q_ref, u, 2 * LANES),
                      vt_ref.at[u], finisher(u)))
    _attend_tiles(units, _mla_first_head, seq, tq)


def _mlp_kernel(x_ref, od_ref, om_ref, p_ref, an_ref, wg_ref, bg_ref, wod_ref, wom_ref, wout_ref,
                fn_ref, wfg_ref, wfu_ref, wfd_ref, pn_ref, wpg_ref, bpg_ref, wple_ref, final_ref,
                o_ref):
    tm = x_ref.shape[0]
    groups = [slice(g * tm // MLP_ROW_GROUPS, (g + 1) * tm // MLP_ROW_GROUPS)
              for g in range(MLP_ROW_GROUPS)]

    def mix(r):
        x = x_ref[r, :]
        h = _rmsnorm(x, an_ref[...]).astype(BF16)
        gates = _sigmoid(_dot(h, wg_ref[...]) + bg_ref[...])
        out_a = _dot(od_ref[r, :], wod_ref[...])
        out_b = _dot(om_ref[r, :], wom_ref[...])
        merged = gates[:, :D_MODEL] * out_a + gates[:, D_MODEL:] * out_b
        return x + _dot(merged.astype(BF16), wout_ref[...])

    def ffn(x):
        h = _rmsnorm(x, fn_ref[...]).astype(BF16)
        gt = _dot(h, wfg_ref[...])
        up = _dot(h, wfu_ref[...])
        hid = (gt * _sigmoid(gt) * up).astype(BF16)
        return x + _dot(hid, wfd_ref[...])

    def embed(x, r):
        h = _rmsnorm(x, pn_ref[...]).astype(BF16)
        gate = _sigmoid(_dot(h, wpg_ref[...]) + bpg_ref[...])
        return x + _dot(p_ref[r, :].astype(BF16), wple_ref[...]) * gate

    xs = [mix(r) for r in groups]
    xs = [ffn(x) for x in xs]
    xs = [embed(x, r) for x, r in zip(xs, groups)]
    for x, r in zip(xs, groups):
        o_ref[r, :] = _rmsnorm(x, final_ref[...])


def _diff_lane_source():
    half, rest = DIFF_ROT_DIM // 2, (DIFF_HEAD_DIM - DIFF_ROT_DIM) // 2
    src = []
    for part in range(2):
        for m in range(2):
            src += [m * DIFF_HEAD_DIM + part * half + j for j in range(half)]
        for m in range(2):
            src += [m * DIFF_HEAD_DIM + DIFF_ROT_DIM + part * rest + j for j in range(rest)]
    return np.array(src)


def _mla_lane_sources():
    half = MLA_ROPE_DIM // 2
    first_nope = LANES // 2 - half
    nope = np.full(LANES, -1)
    rope = np.full(LANES, -1)
    rope[0:half] = np.arange(half)
    nope[half:LANES // 2] = np.arange(first_nope)
    rope[LANES // 2:LANES // 2 + half] = half + np.arange(half)
    rest = MLA_NOPE_DIM - first_nope
    nope[LANES // 2 + half:LANES // 2 + half + rest] = first_nope + np.arange(rest)
    return nope, rope


def _take_cols(w, idx):
    w = jnp.concatenate([w, jnp.zeros((w.shape[0], 1), w.dtype)], axis=1)
    return jnp.take(w, np.where(idx < 0, w.shape[1] - 1, idx), axis=1)


def _rope_tables(seq, rot_dim, theta, freq, passthrough, scale):
    pos = jnp.arange(seq, dtype=F32)
    inv_freq = theta ** (-(jnp.arange(0, rot_dim, 2, dtype=F32) / rot_dim))
    ang = pos[:, None] * inv_freq[None, :]
    cos_l = jnp.take(jnp.cos(ang), np.maximum(freq, 0), axis=1)
    sin_l = jnp.take(jnp.sin(ang), np.maximum(freq, 0), axis=1)
    sign = np.where(np.arange(LANES) < LANES // 2, -1.0, 1.0)
    c = jnp.where(freq >= 0, cos_l, passthrough.astype(np.float32))
    s = jnp.where(freq >= 0, sin_l * sign, 0.0)
    return jnp.stack([c, s]).astype(F32) * scale


def _const_spec(shape):
    nd = len(shape)
    return pl.BlockSpec(shape, lambda *_: (0,) * nd, pipeline_mode=pl.Buffered(1))


def _params(*sem):
    return pltpu.CompilerParams(dimension_semantics=sem, vmem_limit_bytes=VMEM_LIMIT_BYTES)


def kernel(x, p, attn_norm, w_in, b_gate, lam_q1, lam_k1, lam_q2, lam_k2, diff_subln, w_o_diff,
           q_norm, w_uq, kv_norm, w_ukv, w_o_mla, w_out, ffn_norm, w_ffn_gate, w_ffn_up,
           w_ffn_down, ple_norm, w_ple_gate, b_ple_gate, w_ple, final_norm):
    B, S, D = x.shape
    T = B * S
    assert D == D_MODEL and w_in.shape[0] == 1
    x2 = x.reshape(T, D)
    p2 = p[0].reshape(T, PLE_DIM)

    wi = w_in[0]
    o_cq = 3 * D
    o_kr = o_cq + MLA_Q_LORA + MLA_KV_LORA
    o_g = o_kr + MLA_ROPE_DIM
    heads = np.arange(DIFF_HEADS)[:, None]
    diff_src = _diff_lane_source()
    diff_cols = (heads * LANES + diff_src[None, :]).reshape(-1)
    w_qkv = jnp.concatenate([jnp.take(wi[:, :D], diff_cols, axis=1),
                             jnp.take(wi[:, D:2 * D], diff_cols, axis=1),
                             wi[:, 2 * D:o_cq]], axis=1).astype(BF16)
    nope_src, rope_src = _mla_lane_sources()
    w_c = jnp.concatenate([wi[:, o_cq:o_kr], _take_cols(wi[:, o_kr:o_g], rope_src)],
                          axis=1).astype(BF16)
    w_g = wi[:, o_g:].astype(BF16)
    b_g = b_gate[0].reshape(1, 2 * D)

    q_width = MLA_NOPE_DIM + MLA_ROPE_DIM
    q_src = np.where(nope_src >= 0, nope_src, np.where(rope_src >= 0, MLA_NOPE_DIM + rope_src, -1))
    q_cols = np.where(q_src[None, :] >= 0, heads * q_width + q_src[None, :], -1).reshape(-1)
    wuq = _take_cols(w_uq[0], q_cols).astype(BF16)
    kv_width = MLA_NOPE_DIM + MLA_V_DIM
    k_cols = np.where(nope_src[None, :] >= 0, heads * kv_width + nope_src[None, :], -1).reshape(-1)
    v_cols = (heads * kv_width + MLA_NOPE_DIM + np.arange(MLA_V_DIM)[None, :]).reshape(-1)
    wukv = _take_cols(w_ukv[0], np.concatenate([k_cols, v_cols])).astype(BF16)

    d_scale = DIFF_HEAD_DIM ** -0.5 * LOG2E
    m_scale = q_width ** -0.5 * LOG2E
    d_dim = diff_src % DIFF_HEAD_DIM
    d_freq = np.where(d_dim < DIFF_ROT_DIM, d_dim % (DIFF_ROT_DIM // 2), -1)
    ones = np.ones(LANES)
    dq_tab = _rope_tables(S, DIFF_ROT_DIM, ROPE_THETA, d_freq, ones, d_scale)
    dk_tab = _rope_tables(S, DIFF_ROT_DIM, ROPE_THETA, d_freq, ones, 1.0)
    m_freq = np.where(rope_src >= 0, rope_src % (MLA_ROPE_DIM // 2), -1)
    mq_tab = _rope_tables(S, MLA_ROPE_DIM, MLA_ROPE_THETA, m_freq, nope_src >= 0, m_scale)
    mk_tab = _rope_tables(S, MLA_ROPE_DIM, MLA_ROPE_THETA, m_freq, np.zeros(LANES), 1.0)

    row = lambda a: a.reshape(1, -1)

    tm = ROW_TILE_PROJ
    tiles_per_seq = S // tm
    tab_spec = pl.BlockSpec((2, tm, LANES), lambda i: (0, i % tiles_per_seq, 0))
    act = lambda n: pl.BlockSpec((tm, n), lambda i: (i, 0))
    dq, dk, dv, mq, mk, mv = pl.pallas_call(
        _proj_kernel,
        grid=(T // tm,),
        in_specs=[act(D), _const_spec((1, D)), _const_spec(w_qkv.shape), _const_spec(w_c.shape),
                  _const_spec((1, MLA_Q_LORA)), _const_spec(wuq.shape),
                  _const_spec((1, MLA_KV_LORA)), _const_spec(wukv.shape),
                  tab_spec, tab_spec, tab_spec, tab_spec],
        out_specs=[act(D), act(D), act(D), act(D), act(D), act(MLA_HEADS * MLA_V_DIM)],
        out_shape=[jax.ShapeDtypeStruct((T, D), BF16)] * 5
                  + [jax.ShapeDtypeStruct((T, MLA_HEADS * MLA_V_DIM), BF16)],
        compiler_params=_params("parallel"),
        name="proj",
    )(x2, row(attn_norm[0]), w_qkv, w_c, row(q_norm[0]), wuq, row(kv_norm[0]), wukv,
      dq_tab, dk_tab, mq_tab, mk_tab)

    U = ATTN_UNITS
    head_blk = pl.BlockSpec((S, U * LANES), lambda b, h: (b, h))
    lam_spec = _const_spec((1, DIFF_HEAD_DIM))
    od = pl.pallas_call(
        functools.partial(_diff_attn_kernel, seq=S, tq=Q_TILE),
        grid=(B, DIFF_HEADS // U),
        in_specs=[lam_spec, lam_spec, lam_spec, lam_spec, _const_spec((1, DIFF_V_DIM)),
                  head_blk, head_blk, head_blk],
        out_specs=head_blk,
        out_shape=jax.ShapeDtypeStruct((T, DIFF_HEADS * DIFF_V_DIM), BF16),
        scratch_shapes=[pltpu.VMEM((U, DIFF_V_DIM + ONES_ROWS, S), BF16)],
        compiler_params=_params("parallel", "parallel"),
        name="diff_attn",
    )(lam_q1, lam_k1, lam_q2, lam_k2, row(diff_subln[0]), dq, dk, dv)

    pair_blk = pl.BlockSpec((S, 2 * U * LANES), lambda b, g: (b, g))
    om = pl.pallas_call(
        functools.partial(_mla_attn_kernel, seq=S, tq=Q_TILE),
        grid=(B, MLA_HEADS // 2 // U),
        in_specs=[pair_blk, pair_blk, head_blk],
        out_specs=head_blk,
        out_shape=jax.ShapeDtypeStruct((T, MLA_HEADS * MLA_V_DIM), BF16),
        scratch_shapes=[pltpu.VMEM((U, 2 * MLA_V_DIM + ONES_ROWS, S), BF16)],
        compiler_params=_params("parallel", "parallel"),
        name="mla_attn",
    )(mq, mk, mv)

    tm = ROW_TILE_MLP
    act = lambda n: pl.BlockSpec((tm, n), lambda i: (i, 0))
    weights = [w_g, b_g, w_o_diff[0].astype(BF16), w_o_mla[0].astype(BF16), w_out[0].astype(BF16),
               row(ffn_norm[0]), w_ffn_gate[0].astype(BF16), w_ffn_up[0].astype(BF16),
               w_ffn_down[0].astype(BF16), row(ple_norm[0]), w_ple_gate[0].astype(BF16),
               row(b_ple_gate[0]), w_ple[0].astype(BF16), row(final_norm)]
    out = pl.pallas_call(
        _mlp_kernel,
        grid=(T // tm,),
        in_specs=[act(D), act(D), act(MLA_HEADS * MLA_V_DIM), act(PLE_DIM), _const_spec((1, D))]
                 + [_const_spec(w.shape) for w in weights],
        out_specs=act(D),
        out_shape=jax.ShapeDtypeStruct((T, D), F32),
        compiler_params=_params("parallel"),
        name="mlp",
    )(x2, od, om, p2, row(attn_norm[0]), *weights)
    return out.reshape(B, S, D)
```

```python
import functools
import math

import jax
import jax.numpy as jnp
import numpy as np
from jax import lax
from jax.experimental import pallas as pl
from jax.experimental.pallas import tpu as pltpu

D_MODEL = 1024
CHUNK = 64
NORM_EPS = 1e-6
NEG_INF = -1e30

DIFF_HEAD_DIM = 64
DIFF_HEADS = 8
DIFF_V_DIM = 128
DIFF_ROT_DIM = 16
ROPE_THETA = 500000.0
LAM_INIT = 0.8 - 0.6 * math.exp(-0.3 * 0)

MLA_HEADS = 8
MLA_NOPE_DIM = 64
MLA_ROPE_DIM = 32
MLA_V_DIM = 64
MLA_Q_LORA = 384
MLA_KV_LORA = 256
MLA_ROPE_THETA = 10000.0

FFN_HIDDEN = 2816
PLE_DIM = 256

LANES = 128
ONES_ROWS = 16
LOG2E = math.log2(math.e)
VMEM_LIMIT_BYTES = 56 * 1024 * 1024

ROW_TILE_PROJ = 512
ROW_TILE_MLP = 512
MLP_ROW_GROUPS = 2
Q_TILE = 256
KEY_BLOCK = 256
SCORE_LOOKAHEAD = 6
ATTN_UNITS = 1

BF16 = jnp.bfloat16
F32 = jnp.float32


def _rmsnorm(x, g):
    ms = jnp.mean(x * x, axis=-1, keepdims=True)
    return x * lax.rsqrt(ms + NORM_EPS) * g


def _dot(a, b):
    return jnp.dot(a, b, preferred_element_type=F32)


def _dot_nt(a, b):
    return lax.dot_general(a, b, (((1,), (1,)), ((), ())), preferred_element_type=F32)


def _sigmoid(x):
    return 1.0 / (1.0 + jnp.exp(-x))


def _rope_block(x, tab):
    return x * tab[0] + pltpu.roll(x, LANES // 2, 1) * tab[1]


def _proj_kernel(x_ref, an_ref, wqkv_ref, wc_ref, qn_ref, wuq_ref, kvn_ref, wukv_ref,
                 dq_tab, dk_tab, mq_tab, mk_tab,
                 dq_ref, dk_ref, dv_ref, mq_ref, mk_ref, mv_ref):
    h = _rmsnorm(x_ref[...], an_ref[...]).astype(BF16)

    c = _dot(h, wc_ref[...])
    cq = _rmsnorm(c[:, :MLA_Q_LORA], qn_ref[...]).astype(BF16)
    ckv = _rmsnorm(c[:, MLA_Q_LORA:MLA_Q_LORA + MLA_KV_LORA], kvn_ref[...]).astype(BF16)
    kr = _rope_block(c[:, MLA_Q_LORA + MLA_KV_LORA:], mk_tab)

    q = _dot(cq, wuq_ref[...])
    kv = _dot(ckv, wukv_ref[...])
    dq = _dot(h, wqkv_ref[:, 0:D_MODEL])
    for blk in range(MLA_HEADS):
        lo = blk * LANES
        mq_ref[:, lo:lo + LANES] = _rope_block(q[:, lo:lo + LANES], mq_tab).astype(BF16)
        mk_ref[:, lo:lo + LANES] = (kv[:, lo:lo + LANES] + kr).astype(BF16)
    mv_ref[...] = kv[:, MLA_HEADS * LANES:].astype(BF16)

    dk = _dot(h, wqkv_ref[:, D_MODEL:2 * D_MODEL])
    for blk in range(DIFF_HEADS):
        lo = blk * LANES
        dq_ref[:, lo:lo + LANES] = _rope_block(dq[:, lo:lo + LANES], dq_tab).astype(BF16)
    dv = _dot(h, wqkv_ref[:, 2 * D_MODEL:])
    for blk in range(DIFF_HEADS):
        lo = blk * LANES
        dk_ref[:, lo:lo + LANES] = _rope_block(dk[:, lo:lo + LANES], dk_tab).astype(BF16)
    dv_ref[...] = dv.astype(BF16)


def _pair_mask_t(tq):
    key = lax.broadcasted_iota(jnp.int32, (tq, 2 * tq), 0) // CHUNK
    qry = (lax.broadcasted_iota(jnp.int32, (tq, 2 * tq), 1) % tq) // CHUNK
    return key <= qry


def _block_diag_q(q, is_first):
    first = is_first(lax.broadcasted_iota(jnp.int32, q.shape, 1))
    zero = jnp.zeros((), q.dtype)
    return jnp.concatenate([jnp.where(first, q, zero), jnp.where(first, zero, q)], axis=0)


def _diff_first_map(lane):
    within = lane % (LANES // 2)
    half, rest = DIFF_ROT_DIM // 2, (DIFF_HEAD_DIM - DIFF_ROT_DIM) // 2
    return (within < half) | ((within >= 2 * half) & (within < 2 * half + rest))


def _mla_first_head(lane):
    return lane < LANES


def _attend_tiles(units, is_first, seq, tq):
    n = seq // tq
    mask_t = _pair_mask_t(tq)
    chunks = [(u, t, k0, min(k0 + KEY_BLOCK, (t + 1) * tq))
              for u in range(len(units)) for t in range(n)
              for k0 in range(0, (t + 1) * tq, KEY_BLOCK)]
    q_bd = {}

    def scores(u, t, k0, k1):
        k, q = units[u][0], units[u][1]
        if (u, t) not in q_bd:
            q_bd[u, t] = _block_diag_q(q(slice(t * tq, (t + 1) * tq)), is_first)
        s = _dot_nt(k(slice(k0, k1)), q_bd[u, t])
        if k1 < (t + 1) * tq:
            return s
        diag = jnp.where(mask_t, s[k1 - k0 - tq:], NEG_INF)
        return diag if k1 - k0 == tq else jnp.concatenate([s[:k1 - k0 - tq], diag], axis=0)

    pending = [scores(*ch) for ch in chunks[:SCORE_LOOKAHEAD]]
    run_max = acc = None
    for c, (u, t, k0, k1) in enumerate(chunks):
        s = pending.pop(0)
        m = jnp.max(s, axis=0, keepdims=True)
        new_max = m if k0 == 0 else jnp.maximum(run_max, m)
        e = jnp.exp2(s - new_max).astype(BF16)
        part = _dot(units[u][2][:, k0:k1], e)
        if c + SCORE_LOOKAHEAD < len(chunks):
            pending.append(scores(*chunks[c + SCORE_LOOKAHEAD]))
        acc = part if k0 == 0 else acc * jnp.exp2(run_max - new_max) + part
        run_max = new_max
        if k1 == (t + 1) * tq:
            units[u][3](t * tq, acc)


def _fill_vt(vt_ref, v):
    width = v.shape[1]
    vt_ref[0:width, :] = v.T
    vt_ref[width:, :] = jnp.ones((vt_ref.shape[0] - width, vt_ref.shape[1]), BF16)


def _lane_block(ref, u, width):
    return lambda rows: ref[rows, u * width:(u + 1) * width]


def _diff_attn_kernel(lq1_ref, lk1_ref, lq2_ref, lk2_ref, subln_ref, q_ref, k_ref, v_ref, o_ref,
                      vt_ref, *, seq, tq):
    lam = (jnp.exp(jnp.sum(lq1_ref[...] * lk1_ref[...], axis=-1, keepdims=True))
           - jnp.exp(jnp.sum(lq2_ref[...] * lk2_ref[...], axis=-1, keepdims=True))
           + LAM_INIT)
    dv = DIFF_V_DIM

    def finisher(u):
        def finish(lo, a):
            o_t = (a[0:dv, 0:tq] * (1.0 / a[dv:dv + 1, 0:tq])
                   - a[0:dv, tq:] * (lam / a[dv:dv + 1, tq:]))
            o = _rmsnorm(o_t.T, subln_ref[...]) * (1.0 - LAM_INIT)
            o_ref[lo:lo + tq, u * dv:(u + 1) * dv] = o.astype(BF16)
        return finish

    units = []
    for u in range(vt_ref.shape[0]):
        _fill_vt(vt_ref.at[u], v_ref[:, u * dv:(u + 1) * dv])
        units.append((_lane_block(k_ref, u, LANES), _lane_block(q_ref, u, LANES), vt_ref.at[u],
                      finisher(u)))
    _attend_tiles(units, _diff_first_map, seq, tq)


def _mla_attn_kernel(q_ref, k_ref, v_ref, o_ref, vt_ref, *, seq, tq):
    dv = MLA_V_DIM

    def finisher(u):
        def finish(lo, a):
            o_t = jnp.concatenate(
                [a[0:dv, 0:tq] * (1.0 / a[2 * dv:2 * dv + 1, 0:tq]),
                 a[dv:2 * dv, tq:] * (1.0 / a[2 * dv:2 * dv + 1, tq:])], axis=0)
            o_ref[lo:lo + tq, u * LANES:(u + 1) * LANES] = o_t.T.astype(BF16)
        return finish

    units = []
    for u in range(vt_ref.shape[0]):
        _fill_vt(vt_ref.at[u], v_ref[:, u * LANES:(u + 1) * LANES])
        units.append((_lane_block(k_ref, u, 2 * LANES), _lane_block(q_ref, u, 2 * LANES),
                      vt_ref.at[u], finisher(u)))
    _attend_tiles(units, _mla_first_head, seq, tq)


def _mlp_kernel(x_ref, od_ref, om_ref, p_ref, an_ref, wg_ref, bg_ref, wod_ref, wom_ref, wout_ref,
                fn_ref, wfg_ref, wfu_ref, wfd_ref, pn_ref, wpg_ref, bpg_ref, wple_ref, final_ref,
                o_ref):
    tm = x_ref.shape[0]
    groups = [slice(g * tm // MLP_ROW_GROUPS, (g + 1) * tm // MLP_ROW_GROUPS)
              for g in range(MLP_ROW_GROUPS)]

    def mix(r):
        x = x_ref[r, :]
        h = _rmsnorm(x, an_ref[...]).astype(BF16)
        gates = _sigmoid(_dot(h, wg_ref[...]) + bg_ref[...])
        out_a = _dot(od_ref[r, :], wod_ref[...])
        out_b = _dot(om_ref[r, :], wom_ref[...])
        merged = gates[:, :D_MODEL] * out_a + gates[:, D_MODEL:] * out_b
        return x + _dot(merged.astype(BF16), wout_ref[...])

    def ffn(x):
        h = _rmsnorm(x, fn_ref[...]).astype(BF16)
        gt = _dot(h, wfg_ref[...])
        up = _dot(h, wfu_ref[...])
        hid = (gt * _sigmoid(gt) * up).astype(BF16)
        return x + _dot(hid, wfd_ref[...])

    def embed(x, r):
        h = _rmsnorm(x, pn_ref[...]).astype(BF16)
        gate = _sigmoid(_dot(h, wpg_ref[...]) + bpg_ref[...])
        return x + _dot(p_ref[r, :].astype(BF16), wple_ref[...]) * gate

    xs = [mix(r) for r in groups]
    xs = [ffn(x) for x in xs]
    xs = [embed(x, r) for x, r in zip(xs, groups)]
    for x, r in zip(xs, groups):
        o_ref[r, :] = _rmsnorm(x, final_ref[...])


def _diff_lane_source():
    half, rest = DIFF_ROT_DIM // 2, (DIFF_HEAD_DIM - DIFF_ROT_DIM) // 2
    src = []
    for part in range(2):
        for m in range(2):
            src += [m * DIFF_HEAD_DIM + part * half + j for j in range(half)]
        for m in range(2):
            src += [m * DIFF_HEAD_DIM + DIFF_ROT_DIM + part * rest + j for j in range(rest)]
    return np.array(src)


def _mla_lane_sources():
    half = MLA_ROPE_DIM // 2
    first_nope = LANES // 2 - half
    nope = np.full(LANES, -1)
    rope = np.full(LANES, -1)
    rope[0:half] = np.arange(half)
    nope[half:LANES // 2] = np.arange(first_nope)
    rope[LANES // 2:LANES // 2 + half] = half + np.arange(half)
    rest = MLA_NOPE_DIM - first_nope
    nope[LANES // 2 + half:LANES // 2 + half + rest] = first_nope + np.arange(rest)
    return nope, rope


def _take_cols(w, idx):
    w = jnp.concatenate([w, jnp.zeros((w.shape[0], 1), w.dtype)], axis=1)
    return jnp.take(w, np.where(idx < 0, w.shape[1] - 1, idx), axis=1)


def _rope_tables(seq, rot_dim, theta, freq, passthrough, scale):
    pos = jnp.arange(seq, dtype=F32)
    inv_freq = theta ** (-(jnp.arange(0, rot_dim, 2, dtype=F32) / rot_dim))
    ang = pos[:, None] * inv_freq[None, :]
    cos_l = jnp.take(jnp.cos(ang), np.maximum(freq, 0), axis=1)
    sin_l = jnp.take(jnp.sin(ang), np.maximum(freq, 0), axis=1)
    sign = np.where(np.arange(LANES) < LANES // 2, -1.0, 1.0)
    c = jnp.where(freq >= 0, cos_l, passthrough.astype(np.float32))
    s = jnp.where(freq >= 0, sin_l * sign, 0.0)
    return jnp.stack([c, s]).astype(F32) * scale


def _const_spec(shape):
    nd = len(shape)
    return pl.BlockSpec(shape, lambda *_: (0,) * nd, pipeline_mode=pl.Buffered(1))


def _params(*sem):
    return pltpu.CompilerParams(dimension_semantics=sem, vmem_limit_bytes=VMEM_LIMIT_BYTES)


def kernel(x, p, attn_norm, w_in, b_gate, lam_q1, lam_k1, lam_q2, lam_k2, diff_subln, w_o_diff,
           q_norm, w_uq, kv_norm, w_ukv, w_o_mla, w_out, ffn_norm, w_ffn_gate, w_ffn_up,
           w_ffn_down, ple_norm, w_ple_gate, b_ple_gate, w_ple, final_norm):
    B, S, D = x.shape
    T = B * S
    assert D == D_MODEL and w_in.shape[0] == 1
    x2 = x.reshape(T, D)
    p2 = p[0].reshape(T, PLE_DIM)

    wi = w_in[0]
    o_cq = 3 * D
    o_kr = o_cq + MLA_Q_LORA + MLA_KV_LORA
    o_g = o_kr + MLA_ROPE_DIM
    heads = np.arange(DIFF_HEADS)[:, None]
    diff_src = _diff_lane_source()
    diff_cols = (heads * LANES + diff_src[None, :]).reshape(-1)
    w_qkv = jnp.concatenate([jnp.take(wi[:, :D], diff_cols, axis=1),
                             jnp.take(wi[:, D:2 * D], diff_cols, axis=1),
                             wi[:, 2 * D:o_cq]], axis=1).astype(BF16)
    nope_src, rope_src = _mla_lane_sources()
    w_c = jnp.concatenate([wi[:, o_cq:o_kr], _take_cols(wi[:, o_kr:o_g], rope_src)],
                          axis=1).astype(BF16)
    w_g = wi[:, o_g:].astype(BF16)
    b_g = b_gate[0].reshape(1, 2 * D)

    q_width = MLA_NOPE_DIM + MLA_ROPE_DIM
    q_src = np.where(nope_src >= 0, nope_src, np.where(rope_src >= 0, MLA_NOPE_DIM + rope_src, -1))
    q_cols = np.where(q_src[None, :] >= 0, heads * q_width + q_src[None, :], -1).reshape(-1)
    wuq = _take_cols(w_uq[0], q_cols).astype(BF16)
    kv_width = MLA_NOPE_DIM + MLA_V_DIM
    k_cols = np.where(nope_src[None, :] >= 0, heads * kv_width + nope_src[None, :], -1).reshape(-1)
    v_cols = (heads * kv_width + MLA_NOPE_DIM + np.arange(MLA_V_DIM)[None, :]).reshape(-1)
    wukv = _take_cols(w_ukv[0], np.concatenate([k_cols, v_cols])).astype(BF16)

    d_scale = DIFF_HEAD_DIM ** -0.5 * LOG2E
    m_scale = q_width ** -0.5 * LOG2E
    d_dim = diff_src % DIFF_HEAD_DIM
    d_freq = np.where(d_dim < DIFF_ROT_DIM, d_dim % (DIFF_ROT_DIM // 2), -1)
    ones = np.ones(LANES)
    dq_tab = _rope_tables(S, DIFF_ROT_DIM, ROPE_THETA, d_freq, ones, d_scale)
    dk_tab = _rope_tables(S, DIFF_ROT_DIM, ROPE_THETA, d_freq, ones, 1.0)
    m_freq = np.where(rope_src >= 0, rope_src % (MLA_ROPE_DIM // 2), -1)
    mq_tab = _rope_tables(S, MLA_ROPE_DIM, MLA_ROPE_THETA, m_freq, nope_src >= 0, m_scale)
    mk_tab = _rope_tables(S, MLA_ROPE_DIM, MLA_ROPE_THETA, m_freq, np.zeros(LANES), 1.0)

    row = lambda a: a.reshape(1, -1)

    tm = ROW_TILE_PROJ
    tiles_per_seq = S // tm
    tab_spec = pl.BlockSpec((2, tm, LANES), lambda i: (0, i % tiles_per_seq, 0))
    act = lambda n: pl.BlockSpec((tm, n), lambda i: (i, 0))
    dq, dk, dv, mq, mk, mv = pl.pallas_call(
        _proj_kernel,
        grid=(T // tm,),
        in_specs=[act(D), _const_spec((1, D)), _const_spec(w_qkv.shape), _const_spec(w_c.shape),
                  _const_spec((1, MLA_Q_LORA)), _const_spec(wuq.shape),
                  _const_spec((1, MLA_KV_LORA)), _const_spec(wukv.shape),
                  tab_spec, tab_spec, tab_spec, tab_spec],
        out_specs=[act(D), act(D), act(D), act(D), act(D), act(MLA_HEADS * MLA_V_DIM)],
        out_shape=[jax.ShapeDtypeStruct((T, D), BF16)] * 5
                  + [jax.ShapeDtypeStruct((T, MLA_HEADS * MLA_V_DIM), BF16)],
        compiler_params=_params("parallel"),
        name="proj",
    )(x2, row(attn_norm[0]), w_qkv, w_c, row(q_norm[0]), wuq, row(kv_norm[0]), wukv,
      dq_tab, dk_tab, mq_tab, mk_tab)

    U = ATTN_UNITS
    head_blk = pl.BlockSpec((S, U * LANES), lambda b, h: (b, h))
    lam_spec = _const_spec((1, DIFF_HEAD_DIM))
    od = pl.pallas_call(
        functools.partial(_diff_attn_kernel, seq=S, tq=Q_TILE),
        grid=(B, DIFF_HEADS // U),
        in_specs=[lam_spec, lam_spec, lam_spec, lam_spec, _const_spec((1, DIFF_V_DIM)),
                  head_blk, head_blk, head_blk],
        out_specs=head_blk,
        out_shape=jax.ShapeDtypeStruct((T, DIFF_HEADS * DIFF_V_DIM), BF16),
        scratch_shapes=[pltpu.VMEM((U, DIFF_V_DIM + ONES_ROWS, S), BF16)],
        compiler_params=_params("parallel", "parallel"),
        name="diff_attn",
    )(lam_q1, lam_k1, lam_q2, lam_k2, row(diff_subln[0]), dq, dk, dv)

    pair_blk = pl.BlockSpec((S, 2 * U * LANES), lambda b, g: (b, g))
    om = pl.pallas_call(
        functools.partial(_mla_attn_kernel, seq=S, tq=Q_TILE),
        grid=(B, MLA_HEADS // 2 // U),
        in_specs=[pair_blk, pair_blk, head_blk],
        out_specs=head_blk,
        out_shape=jax.ShapeDtypeStruct((T, MLA_HEADS * MLA_V_DIM), BF16),
        scratch_shapes=[pltpu.VMEM((U, 2 * MLA_V_DIM + ONES_ROWS, S), BF16)],
        compiler_params=_params("parallel", "parallel"),
        name="mla_attn",
    )(mq, mk, mv)

    tm = ROW_TILE_MLP
    act = lambda n: pl.BlockSpec((tm, n), lambda i: (i, 0))
    weights = [w_g, b_g, w_o_diff[0].astype(BF16), w_o_mla[0].astype(BF16), w_out[0].astype(BF16),
               row(ffn_norm[0]), w_ffn_gate[0].astype(BF16), w_ffn_up[0].astype(BF16),
               w_ffn_down[0].astype(BF16), row(ple_norm[0]), w_ple_gate[0].astype(BF16),
               row(b_ple_gate[0]), w_ple[0].astype(BF16), row(final_norm)]
    out = pl.pallas_call(
        _mlp_kernel,
        grid=(T // tm,),
        in_specs=[act(D), act(D), act(MLA_HEADS * MLA_V_DIM), act(PLE_DIM), _const_spec((1, D))]
                 + [_const_spec(w.shape) for w in weights],
        out_specs=act(D),
        out_shape=jax.ShapeDtypeStruct((T, D), F32),
        compiler_params=_params("parallel"),
        name="mlp",
    )(x2, od, om, p2, row(attn_norm[0]), *weights)
    return out.reshape(B, S, D)
```

```python
import functools
import math

import jax
import jax.numpy as jnp
import numpy as np
from jax import lax
from jax.experimental import pallas as pl
from jax.experimental.pallas import tpu as pltpu

D_MODEL = 1024
CHUNK = 64
NORM_EPS = 1e-6
NEG_INF = -1e30

DIFF_HEAD_DIM = 64
DIFF_HEADS = 8
DIFF_V_DIM = 128
DIFF_ROT_DIM = 16
ROPE_THETA = 500000.0
LAM_INIT = 0.8 - 0.6 * math.exp(-0.3 * 0)

MLA_HEADS = 8
MLA_NOPE_DIM = 64
MLA_ROPE_DIM = 32
MLA_V_DIM = 64
MLA_Q_LORA = 384
MLA_KV_LORA = 256
MLA_ROPE_THETA = 10000.0

FFN_HIDDEN = 2816
PLE_DIM = 256

LANES = 128
ONES_ROWS = 16
LOG2E = math.log2(math.e)
VMEM_LIMIT_BYTES = 56 * 1024 * 1024

ROW_TILE_PROJ = 512
ROW_TILE_MLP = 512
MLP_ROW_GROUPS = 2
Q_TILE = 256
MXU_DEPTH = 256
DIFF_REF_LANE = LANES
MLA_REF_LANE = MLA_NOPE_DIM + MLA_ROPE_DIM
REF_MARGIN = 100.0

BF16 = jnp.bfloat16
F32 = jnp.float32


def _rmsnorm(x, g):
    ms = jnp.mean(x * x, axis=-1, keepdims=True)
    return x * lax.rsqrt(ms + NORM_EPS) * g


def _dot(a, b):
    return jnp.dot(a, b, preferred_element_type=F32)


def _dot_nt(a, b):
    return lax.dot_general(a, b, (((1,), (1,)), ((), ())), preferred_element_type=F32)


def _sigmoid(x):
    return 1.0 / (1.0 + jnp.exp(-x))


def _rope_block(x, tab):
    return x * tab[0] + pltpu.roll(x, LANES // 2, 1) * tab[1]


def _proj_kernel(x_ref, an_ref, wqkv_ref, wc_ref, qn_ref, wuq_ref, kvn_ref, wukv_ref,
                 dq_tab, dk_tab, mq_tab, mk_tab,
                 dq_ref, dk_ref, dv_ref, mq_ref, mk_ref, mv_ref):
    h = _rmsnorm(x_ref[...], an_ref[...]).astype(BF16)

    c = _dot(h, wc_ref[...])
    cq = _rmsnorm(c[:, :MLA_Q_LORA], qn_ref[...]).astype(BF16)
    ckv = _rmsnorm(c[:, MLA_Q_LORA:MLA_Q_LORA + MLA_KV_LORA], kvn_ref[...]).astype(BF16)
    kr = _rope_block(c[:, MLA_Q_LORA + MLA_KV_LORA:], mk_tab)

    q = _dot(cq, wuq_ref[...])
    kv = _dot(ckv, wukv_ref[...])
    dq = _dot(h, wqkv_ref[:, 0:D_MODEL])
    for blk in range(MLA_HEADS):
        lo = blk * LANES
        mq_ref[:, lo:lo + LANES] = _rope_block(q[:, lo:lo + LANES], mq_tab).astype(BF16)
        mk_ref[:, lo:lo + LANES] = (kv[:, lo:lo + LANES] + kr).astype(BF16)
    mv_ref[...] = kv[:, MLA_HEADS * LANES:].astype(BF16)

    dk = _dot(h, wqkv_ref[:, D_MODEL:2 * D_MODEL])
    for blk in range(DIFF_HEADS):
        lo = blk * LANES
        dq_ref[:, lo:lo + LANES] = _rope_block(dq[:, lo:lo + LANES], dq_tab).astype(BF16)
    dv = _dot(h, wqkv_ref[:, 2 * D_MODEL:])
    for blk in range(DIFF_HEADS):
        lo = blk * LANES
        dk_ref[:, lo:lo + LANES] = _rope_block(dk[:, lo:lo + LANES], dk_tab).astype(BF16)
    dv_ref[...] = dv.astype(BF16)


def _pair_mask_t(tq):
    key = lax.broadcasted_iota(jnp.int32, (tq, 2 * tq), 0) // CHUNK
    qry = (lax.broadcasted_iota(jnp.int32, (tq, 2 * tq), 1) % tq) // CHUNK
    return key <= qry


def _block_diag_q(q, is_first):
    first = is_first(lax.broadcasted_iota(jnp.int32, q.shape, 1))
    zero = jnp.zeros((), q.dtype)
    return jnp.concatenate([jnp.where(first, q, zero), jnp.where(first, zero, q)], axis=0)


def _diff_first_map(lane):
    within = lane % (LANES // 2)
    half, rest = DIFF_ROT_DIM // 2, (DIFF_HEAD_DIM - DIFF_ROT_DIM) // 2
    return (within < half) | ((within >= 2 * half) & (within < 2 * half + rest))


def _mla_first_head(lane):
    return lane < LANES


def _attend_tiles(ka_ref, q_ref, is_first, one_lane, vt_ref, corr_ref, seq, tq, finish):
    n = seq // tq
    mask_t = _pair_mask_t(tq)
    lane = lax.broadcasted_iota(jnp.int32, (2 * tq, ka_ref.shape[1]), 1)
    hi_lane, lo_lane = lane == one_lane, lane == one_lane + 1
    alt_rows = lax.broadcasted_iota(jnp.int32, (LANES, 2 * tq), 0) % 2 == 0

    def diag_stage(t):
        rows = slice(t * tq, (t + 1) * tq)
        q_bd = _block_diag_q(q_ref[rows, :], is_first)
        if q_bd.shape[1] < ka_ref.shape[1]:
            q_bd = jnp.concatenate(
                [q_bd, jnp.zeros((2 * tq, ka_ref.shape[1] - q_bd.shape[1]), BF16)], axis=1)
        s_d = jnp.where(mask_t, _dot_nt(ka_ref[rows, :], q_bd), NEG_INF)
        m_d = jnp.max(s_d, axis=0, keepdims=True)
        base = m_d + corr_ref[t, 0:1, :]
        hi = base.astype(BF16).astype(F32)
        lo = (base - hi).astype(BF16).astype(F32)
        ref = hi + lo
        cols = jnp.where(alt_rows, hi, lo).T
        q_ref_pt = jnp.where(hi_lane, (-cols[:, 0:1]).astype(BF16),
                             jnp.where(lo_lane, (-cols[:, 1:2]).astype(BF16), q_bd))
        return s_d, ref, q_ref_pt, m_d - ref

    def one_pass():
        worst = None
        stages = [diag_stage(t) for t in range(n)]
        for t in range(n):
            s_d, ref, q_ref_pt, over = stages[t]
            e_all = jnp.exp2(s_d - ref).astype(BF16)
            if t > 0:
                s_rel = _dot_nt(ka_ref[0:t * tq, :], q_ref_pt)
                over = jnp.maximum(over, jnp.max(s_rel, axis=0, keepdims=True))
                e_all = jnp.concatenate([jnp.exp2(s_rel).astype(BF16), e_all], axis=0)
            finish(t * tq, _dot(vt_ref[:, 0:(t + 1) * tq], e_all))
            corr_ref[t] = jnp.broadcast_to(corr_ref[t, 0:1, :] + jnp.maximum(over, 0.0),
                                           corr_ref.shape[1:])
            worst = over if worst is None else jnp.maximum(worst, over)
        return (jnp.max(worst) > REF_MARGIN).astype(jnp.int32)

    corr_ref[...] = jnp.zeros(corr_ref.shape, F32)
    lax.while_loop(lambda c: (c[0] == 0) | ((c[0] == 1) & (c[1] > 0)),
                   lambda c: (c[0] + 1, one_pass()), (jnp.int32(0), jnp.int32(0)))


def _fill_vt(vt_ref, v):
    width = v.shape[1]
    vt_ref[0:width, :] = v.T
    vt_ref[width:, :] = jnp.ones((vt_ref.shape[0] - width, vt_ref.shape[1]), BF16)


def _fill_keys(ka_ref, k_ref, one_lane):
    lane = lax.broadcasted_iota(jnp.int32, ka_ref.shape, 1)
    is_one = (lane == one_lane) | (lane == one_lane + 1)
    k = k_ref[...]
    if k.shape[1] < ka_ref.shape[1]:
        k = jnp.concatenate([k, jnp.zeros((k.shape[0], ka_ref.shape[1] - k.shape[1]), BF16)], axis=1)
    ka_ref[...] = jnp.where(is_one, jnp.ones((), BF16), k)


def _diff_attn_kernel(lq1_ref, lk1_ref, lq2_ref, lk2_ref, subln_ref, q_ref, k_ref, v_ref, o_ref,
                      vt_ref, ka_ref, corr_ref, *, seq, tq):
    lam = (jnp.exp(jnp.sum(lq1_ref[...] * lk1_ref[...], axis=-1, keepdims=True))
           - jnp.exp(jnp.sum(lq2_ref[...] * lk2_ref[...], axis=-1, keepdims=True))
           + LAM_INIT)
    dv = DIFF_V_DIM
    _fill_vt(vt_ref, v_ref[...])
    _fill_keys(ka_ref, k_ref, DIFF_REF_LANE)

    def finish(lo, a):
        o_t = (a[0:dv, 0:tq] * (1.0 / a[dv:dv + 1, 0:tq])
               - a[0:dv, tq:] * (lam / a[dv:dv + 1, tq:]))
        o = _rmsnorm(o_t.T, subln_ref[...]) * (1.0 - LAM_INIT)
        o_ref[lo:lo + tq, :] = o.astype(BF16)

    _attend_tiles(ka_ref, q_ref, _diff_first_map, DIFF_REF_LANE, vt_ref, corr_ref, seq, tq, finish)


def _mla_attn_kernel(q_ref, k_ref, v_ref, o_ref, vt_ref, ka_ref, corr_ref, *, seq, tq):
    dv = MLA_V_DIM
    _fill_vt(vt_ref, v_ref[...])
    _fill_keys(ka_ref, k_ref, MLA_REF_LANE)

    def finish(lo, a):
        o_t = jnp.concatenate(
            [a[0:dv, 0:tq] * (1.0 / a[2 * dv:2 * dv + 1, 0:tq]),
             a[dv:2 * dv, tq:] * (1.0 / a[2 * dv:2 * dv + 1, tq:])], axis=0)
        o_ref[lo:lo + tq, :] = o_t.T.astype(BF16)

    _attend_tiles(ka_ref, q_ref, _mla_first_head, MLA_REF_LANE, vt_ref, corr_ref, seq, tq, finish)


def _mlp_kernel(x_ref, od_ref, om_ref, p_ref, an_ref, wg_ref, bg_ref, wod_ref, wom_ref, wout_ref,
                fn_ref, wfg_ref, wfu_ref, wfd_ref, pn_ref, wpg_ref, bpg_ref, wple_ref, final_ref,
                o_ref):
    tm = x_ref.shape[0]
    groups = [slice(g * tm // MLP_ROW_GROUPS, (g + 1) * tm // MLP_ROW_GROUPS)
              for g in range(MLP_ROW_GROUPS)]

    def mix(r):
        x = x_ref[r, :]
        h = _rmsnorm(x, an_ref[...]).astype(BF16)
        gates = _sigmoid(_dot(h, wg_ref[...]) + bg_ref[...])
        out_a = _dot(od_ref[r, :], wod_ref[...])
        out_b = _dot(om_ref[r, :], wom_ref[...])
        merged = gates[:, :D_MODEL] * out_a + gates[:, D_MODEL:] * out_b
        return x + _dot(merged.astype(BF16), wout_ref[...])

    def ffn(x):
        h = _rmsnorm(x, fn_ref[...]).astype(BF16)
        gt = _dot(h, wfg_ref[...])
        up = _dot(h, wfu_ref[...])
        hid = (gt * _sigmoid(gt) * up).astype(BF16)
        return x + _dot(hid, wfd_ref[...])

    def embed(x, r):
        h = _rmsnorm(x, pn_ref[...]).astype(BF16)
        gate = _sigmoid(_dot(h, wpg_ref[...]) + bpg_ref[...])
        return x + _dot(p_ref[r, :].astype(BF16), wple_ref[...]) * gate

    xs = [mix(r) for r in groups]
    xs = [ffn(x) for x in xs]
    xs = [embed(x, r) for x, r in zip(xs, groups)]
    for x, r in zip(xs, groups):
        o_ref[r, :] = _rmsnorm(x, final_ref[...])


def _diff_lane_source():
    half, rest = DIFF_ROT_DIM // 2, (DIFF_HEAD_DIM - DIFF_ROT_DIM) // 2
    src = []
    for part in range(2):
        for m in range(2):
            src += [m * DIFF_HEAD_DIM + part * half + j for j in range(half)]
        for m in range(2):
            src += [m * DIFF_HEAD_DIM + DIFF_ROT_DIM + part * rest + j for j in range(rest)]
    return np.array(src)


def _mla_lane_sources():
    half = MLA_ROPE_DIM // 2
    first_nope = LANES // 2 - half
    nope = np.full(LANES, -1)
    rope = np.full(LANES, -1)
    rope[0:half] = np.arange(half)
    nope[half:LANES // 2] = np.arange(first_nope)
    rope[LANES // 2:LANES // 2 + half] = half + np.arange(half)
    rest = MLA_NOPE_DIM - first_nope
    nope[LANES // 2 + half:LANES // 2 + half + rest] = first_nope + np.arange(rest)
    return nope, rope


def _take_cols(w, idx):
    w = jnp.concatenate([w, jnp.zeros((w.shape[0], 1), w.dtype)], axis=1)
    return jnp.take(w, np.where(idx < 0, w.shape[1] - 1, idx), axis=1)


def _rope_tables(seq, rot_dim, theta, freq, passthrough, scale):
    pos = jnp.arange(seq, dtype=F32)
    inv_freq = theta ** (-(jnp.arange(0, rot_dim, 2, dtype=F32) / rot_dim))
    ang = pos[:, None] * inv_freq[None, :]
    cos_l = jnp.take(jnp.cos(ang), np.maximum(freq, 0), axis=1)
    sin_l = jnp.take(jnp.sin(ang), np.maximum(freq, 0), axis=1)
    sign = np.where(np.arange(LANES) < LANES // 2, -1.0, 1.0)
    c = jnp.where(freq >= 0, cos_l, passthrough.astype(np.float32))
    s = jnp.where(freq >= 0, sin_l * sign, 0.0)
    return jnp.stack([c, s]).astype(F32) * scale


def _const_spec(shape):
    nd = len(shape)
    return pl.BlockSpec(shape, lambda *_: (0,) * nd, pipeline_mode=pl.Buffered(1))


def _params(*sem):
    return pltpu.CompilerParams(dimension_semantics=sem, vmem_limit_bytes=VMEM_LIMIT_BYTES)


def kernel(x, p, attn_norm, w_in, b_gate, lam_q1, lam_k1, lam_q2, lam_k2, diff_subln, w_o_diff,
           q_norm, w_uq, kv_norm, w_ukv, w_o_mla, w_out, ffn_norm, w_ffn_gate, w_ffn_up,
           w_ffn_down, ple_norm, w_ple_gate, b_ple_gate, w_ple, final_norm):
    B, S, D = x.shape
    T = B * S
    assert D == D_MODEL and w_in.shape[0] == 1
    x2 = x.reshape(T, D)
    p2 = p[0].reshape(T, PLE_DIM)

    wi = w_in[0]
    o_cq = 3 * D
    o_kr = o_cq + MLA_Q_LORA + MLA_KV_LORA
    o_g = o_kr + MLA_ROPE_DIM
    heads = np.arange(DIFF_HEADS)[:, None]
    diff_src = _diff_lane_source()
    diff_cols = (heads * LANES + diff_src[None, :]).reshape(-1)
    w_qkv = jnp.concatenate([jnp.take(wi[:, :D], diff_cols, axis=1),
                             jnp.take(wi[:, D:2 * D], diff_cols, axis=1),
                             wi[:, 2 * D:o_cq]], axis=1).astype(BF16)
    nope_src, rope_src = _mla_lane_sources()
    w_c = jnp.concatenate([wi[:, o_cq:o_kr], _take_cols(wi[:, o_kr:o_g], rope_src)],
                          axis=1).astype(BF16)
    w_g = wi[:, o_g:].astype(BF16)
    b_g = b_gate[0].reshape(1, 2 * D)

    q_width = MLA_NOPE_DIM + MLA_ROPE_DIM
    q_src = np.where(nope_src >= 0, nope_src, np.where(rope_src >= 0, MLA_NOPE_DIM + rope_src, -1))
    q_cols = np.where(q_src[None, :] >= 0, heads * q_width + q_src[None, :], -1).reshape(-1)
    wuq = _take_cols(w_uq[0], q_cols).astype(BF16)
    kv_width = MLA_NOPE_DIM + MLA_V_DIM
    k_cols = np.where(nope_src[None, :] >= 0, heads * kv_width + nope_src[None, :], -1).reshape(-1)
    v_cols = (heads * kv_width + MLA_NOPE_DIM + np.arange(MLA_V_DIM)[None, :]).reshape(-1)
    wukv = _take_cols(w_ukv[0], np.concatenate([k_cols, v_cols])).astype(BF16)

    d_scale = DIFF_HEAD_DIM ** -0.5 * LOG2E
    m_scale = q_width ** -0.5 * LOG2E
    d_dim = diff_src % DIFF_HEAD_DIM
    d_freq = np.where(d_dim < DIFF_ROT_DIM, d_dim % (DIFF_ROT_DIM // 2), -1)
    ones = np.ones(LANES)
    dq_tab = _rope_tables(S, DIFF_ROT_DIM, ROPE_THETA, d_freq, ones, d_scale)
    dk_tab = _rope_tables(S, DIFF_ROT_DIM, ROPE_THETA, d_freq, ones, 1.0)
    m_freq = np.where(rope_src >= 0, rope_src % (MLA_ROPE_DIM // 2), -1)
    mq_tab = _rope_tables(S, MLA_ROPE_DIM, MLA_ROPE_THETA, m_freq, nope_src >= 0, m_scale)
    mk_tab = _rope_tables(S, MLA_ROPE_DIM, MLA_ROPE_THETA, m_freq, np.zeros(LANES), 1.0)

    row = lambda a: a.reshape(1, -1)

    tm = ROW_TILE_PROJ
    tiles_per_seq = S // tm
    tab_spec = pl.BlockSpec((2, tm, LANES), lambda i: (0, i % tiles_per_seq, 0))
    act = lambda n: pl.BlockSpec((tm, n), lambda i: (i, 0))
    dq, dk, dv, mq, mk, mv = pl.pallas_call(
        _proj_kernel,
        grid=(T // tm,),
        in_specs=[act(D), _const_spec((1, D)), _const_spec(w_qkv.shape), _const_spec(w_c.shape),
                  _const_spec((1, MLA_Q_LORA)), _const_spec(wuq.shape),
                  _const_spec((1, MLA_KV_LORA)), _const_spec(wukv.shape),
                  tab_spec, tab_spec, tab_spec, tab_spec],
        out_specs=[act(D), act(D), act(D), act(D), act(D), act(MLA_HEADS * MLA_V_DIM)],
        out_shape=[jax.ShapeDtypeStruct((T, D), BF16)] * 5
                  + [jax.ShapeDtypeStruct((T, MLA_HEADS * MLA_V_DIM), BF16)],
        compiler_params=_params("parallel"),
        name="proj",
    )(x2, row(attn_norm[0]), w_qkv, w_c, row(q_norm[0]), wuq, row(kv_norm[0]), wukv,
      dq_tab, dk_tab, mq_tab, mk_tab)

    head_blk = pl.BlockSpec((S, LANES), lambda b, h: (b, h))
    attn_scratch = [pltpu.VMEM((S, MXU_DEPTH), BF16),
                    pltpu.VMEM((S // Q_TILE, 8, 2 * Q_TILE), F32)]
    lam_spec = _const_spec((1, DIFF_HEAD_DIM))
    od = pl.pallas_call(
        functools.partial(_diff_attn_kernel, seq=S, tq=Q_TILE),
        grid=(B, DIFF_HEADS),
        in_specs=[lam_spec, lam_spec, lam_spec, lam_spec, _const_spec((1, DIFF_V_DIM)),
                  head_blk, head_blk, head_blk],
        out_specs=head_blk,
        out_shape=jax.ShapeDtypeStruct((T, DIFF_HEADS * DIFF_V_DIM), BF16),
        scratch_shapes=[pltpu.VMEM((DIFF_V_DIM + ONES_ROWS, S), BF16)] + attn_scratch,
        compiler_params=_params("parallel", "parallel"),
        name="diff_attn",
    )(lam_q1, lam_k1, lam_q2, lam_k2, row(diff_subln[0]), dq, dk, dv)

    pair_blk = pl.BlockSpec((S, 2 * LANES), lambda b, g: (b, g))
    om = pl.pallas_call(
        functools.partial(_mla_attn_kernel, seq=S, tq=Q_TILE),
        grid=(B, MLA_HEADS // 2),
        in_specs=[pair_blk, pair_blk, head_blk],
        out_specs=head_blk,
        out_shape=jax.ShapeDtypeStruct((T, MLA_HEADS * MLA_V_DIM), BF16),
        scratch_shapes=[pltpu.VMEM((2 * MLA_V_DIM + ONES_ROWS, S), BF16)] + attn_scratch,
        compiler_params=_params("parallel", "parallel"),
        name="mla_attn",
    )(mq, mk, mv)

    tm = ROW_TILE_MLP
    act = lambda n: pl.BlockSpec((tm, n), lambda i: (i, 0))
    weights = [w_g, b_g, w_o_diff[0].astype(BF16), w_o_mla[0].astype(BF16), w_out[0].astype(BF16),
               row(ffn_norm[0]), w_ffn_gate[0].astype(BF16), w_ffn_up[0].astype(BF16),
               w_ffn_down[0].astype(BF16), row(ple_norm[0]), w_ple_gate[0].astype(BF16),
               row(b_ple_gate[0]), w_ple[0].astype(BF16), row(final_norm)]
    out = pl.pallas_call(
        _mlp_kernel,
        grid=(T // tm,),
        in_specs=[act(D), act(D), act(MLA_HEADS * MLA_V_DIM), act(PLE_DIM), _const_spec((1, D))]
                 + [_const_spec(w.shape) for w in weights],
        out_specs=act(D),
        out_shape=jax.ShapeDtypeStruct((T, D), F32),
        compiler_params=_params("parallel"),
        name="mlp",
    )(x2, od, om, p2, row(attn_norm[0]), *weights)
    return out.reshape(B, S, D)
```

```python
import functools
import math

import jax
import jax.numpy as jnp
import numpy as np
from jax import lax
from jax.experimental import pallas as pl
from jax.experimental.pallas import tpu as pltpu

D_MODEL = 1024
CHUNK = 64
NORM_EPS = 1e-6
NEG_INF = -1e30

DIFF_HEAD_DIM = 64
DIFF_HEADS = 8
DIFF_V_DIM = 128
DIFF_ROT_DIM = 16
ROPE_THETA = 500000.0
LAM_INIT = 0.8 - 0.6 * math.exp(-0.3 * 0)

MLA_HEADS = 8
MLA_NOPE_DIM = 64
MLA_ROPE_DIM = 32
MLA_V_DIM = 64
MLA_Q_LORA = 384
MLA_KV_LORA = 256
MLA_ROPE_THETA = 10000.0

FFN_HIDDEN = 2816
PLE_DIM = 256

LANES = 128
ONES_ROWS = 16
LOG2E = math.log2(math.e)
VMEM_LIMIT_BYTES = 56 * 1024 * 1024

ROW_TILE_PROJ = 512
ROW_TILE_MLP = 512
MLP_ROW_GROUPS = 2
Q_TILE = 256
SCORE_LOOKAHEAD = 6
MLA_REF_LANE = MLA_NOPE_DIM + MLA_ROPE_DIM
REF_MARGIN = 100.0

BF16 = jnp.bfloat16
F32 = jnp.float32


def _rmsnorm(x, g):
    ms = jnp.mean(x * x, axis=-1, keepdims=True)
    return x * lax.rsqrt(ms + NORM_EPS) * g


def _dot(a, b):
    return jnp.dot(a, b, preferred_element_type=F32)


def _dot_nt(a, b):
    return lax.dot_general(a, b, (((1,), (1,)), ((), ())), preferred_element_type=F32)


def _sigmoid(x):
    return 1.0 / (1.0 + jnp.exp(-x))


def _rope_block(x, tab):
    return x * tab[0] + pltpu.roll(x, LANES // 2, 1) * tab[1]


def _proj_kernel(x_ref, an_ref, wqkv_ref, wc_ref, qn_ref, wuq_ref, kvn_ref, wukv_ref,
                 dq_tab, dk_tab, mq_tab, mk_tab,
                 dq_ref, dk_ref, dv_ref, mq_ref, mk_ref, mv_ref):
    h = _rmsnorm(x_ref[...], an_ref[...]).astype(BF16)

    c = _dot(h, wc_ref[...])
    cq = _rmsnorm(c[:, :MLA_Q_LORA], qn_ref[...]).astype(BF16)
    ckv = _rmsnorm(c[:, MLA_Q_LORA:MLA_Q_LORA + MLA_KV_LORA], kvn_ref[...]).astype(BF16)
    kr = _rope_block(c[:, MLA_Q_LORA + MLA_KV_LORA:], mk_tab)

    q = _dot(cq, wuq_ref[...])
    kv = _dot(ckv, wukv_ref[...])
    dq = _dot(h, wqkv_ref[:, 0:D_MODEL])
    for blk in range(MLA_HEADS):
        lo = blk * LANES
        mq_ref[:, lo:lo + LANES] = _rope_block(q[:, lo:lo + LANES], mq_tab).astype(BF16)
        mk_ref[:, lo:lo + LANES] = (kv[:, lo:lo + LANES] + kr).astype(BF16)
    mv_ref[...] = kv[:, MLA_HEADS * LANES:].astype(BF16)

    dk = _dot(h, wqkv_ref[:, D_MODEL:2 * D_MODEL])
    for blk in range(DIFF_HEADS):
        lo = blk * LANES
        dq_ref[:, lo:lo + LANES] = _rope_block(dq[:, lo:lo + LANES], dq_tab).astype(BF16)
    dv = _dot(h, wqkv_ref[:, 2 * D_MODEL:])
    for blk in range(DIFF_HEADS):
        lo = blk * LANES
        dk_ref[:, lo:lo + LANES] = _rope_block(dk[:, lo:lo + LANES], dk_tab).astype(BF16)
    dv_ref[...] = dv.astype(BF16)


def _pair_mask_t(tq):
    key = lax.broadcasted_iota(jnp.int32, (tq, 2 * tq), 0) // CHUNK
    qry = (lax.broadcasted_iota(jnp.int32, (tq, 2 * tq), 1) % tq) // CHUNK
    return key <= qry


def _block_diag_q(q, is_first):
    first = is_first(lax.broadcasted_iota(jnp.int32, q.shape, 1))
    zero = jnp.zeros((), q.dtype)
    return jnp.concatenate([jnp.where(first, q, zero), jnp.where(first, zero, q)], axis=0)


def _diff_first_map(lane):
    within = lane % (LANES // 2)
    half, rest = DIFF_ROT_DIM // 2, (DIFF_HEAD_DIM - DIFF_ROT_DIM) // 2
    return (within < half) | ((within >= 2 * half) & (within < 2 * half + rest))


def _mla_first_head(lane):
    return lane < LANES


def _attend_tiles_online(k_ref, q_ref, is_first, vt_ref, seq, tq, finish):
    n = seq // tq
    mask_t = _pair_mask_t(tq)
    chunks = [(t, kb) for t in range(n) for kb in range(t + 1)]
    q_bd = {}

    def scores(t, kb):
        if t not in q_bd:
            q_bd[t] = _block_diag_q(q_ref[t * tq:(t + 1) * tq, :], is_first)
        s = _dot_nt(k_ref[kb * tq:(kb + 1) * tq, :], q_bd[t])
        return jnp.where(mask_t, s, NEG_INF) if kb == t else s

    pending = [scores(*ch) for ch in chunks[:SCORE_LOOKAHEAD]]
    run_max = acc = None
    for c, (t, kb) in enumerate(chunks):
        s = pending.pop(0)
        m = jnp.max(s, axis=0, keepdims=True)
        new_max = m if kb == 0 else jnp.maximum(run_max, m)
        e = jnp.exp2(s - new_max).astype(BF16)
        part = _dot(vt_ref[:, kb * tq:(kb + 1) * tq], e)
        if c + SCORE_LOOKAHEAD < len(chunks):
            pending.append(scores(*chunks[c + SCORE_LOOKAHEAD]))
        acc = part if kb == 0 else acc * jnp.exp2(run_max - new_max) + part
        run_max = new_max
        if kb == t:
            finish(t * tq, acc)


def _attend_tiles_ref_point(ka_ref, q_ref, is_first, one_lane, vt_ref, corr_ref, seq, tq, finish):
    n = seq // tq
    mask_t = _pair_mask_t(tq)
    lane = lax.broadcasted_iota(jnp.int32, (2 * tq, ka_ref.shape[1]), 1)
    hi_lane, lo_lane = lane == one_lane, lane == one_lane + 1
    alt_rows = lax.broadcasted_iota(jnp.int32, (LANES, 2 * tq), 0) % 2 == 0

    def diag_stage(t):
        rows = slice(t * tq, (t + 1) * tq)
        q_bd = _block_diag_q(q_ref[rows, :], is_first)
        s_d =jnp.where(mask_t, _dot_nt(ka_ref[rows, :], q_bd), NEG_INF)
        m_d = jnp.max(s_d, axis=0, keepdims=True)
        base = m_d + corr_ref[t, 0:1, :]
        hi = base.astype(BF16).astype(F32)
        lo = (base - hi).astype(BF16).astype(F32)
        ref = hi + lo
        cols = jnp.where(alt_rows, hi, lo).T
        q_ref_pt = jnp.where(hi_lane, (-cols[:, 0:1]).astype(BF16),
                             jnp.where(lo_lane, (-cols[:, 1:2]).astype(BF16), q_bd))
        return s_d, ref, q_ref_pt, m_d - ref

    def one_pass():
        worst = None
        stages = [diag_stage(t) for t in range(n)]
        for t in range(n):
            s_d, ref, q_ref_pt, over = stages[t]
            e_all = jnp.exp2(s_d - ref).astype(BF16)
            if t > 0:
                s_rel = _dot_nt(ka_ref[0:t * tq, :], q_ref_pt)
                over = jnp.maximum(over, jnp.max(s_rel, axis=0, keepdims=True))
                e_all = jnp.concatenate([jnp.exp2(s_rel).astype(BF16), e_all], axis=0)
            finish(t * tq, _dot(vt_ref[:, 0:(t + 1) * tq], e_all))
            corr_ref[t] = jnp.broadcast_to(corr_ref[t, 0:1, :] + jnp.maximum(over, 0.0),
                                           corr_ref.shape[1:])
            worst = over if worst is None else jnp.maximum(worst, over)
        return (jnp.max(worst) > REF_MARGIN).astype(jnp.int32)

    corr_ref[...] = jnp.zeros(corr_ref.shape, F32)
    lax.while_loop(lambda c: (c[0] == 0) | ((c[0] == 1) & (c[1] > 0)),
                   lambda c: (c[0] + 1, one_pass()), (jnp.int32(0), jnp.int32(0)))


def _fill_vt(vt_ref, v):
    width = v.shape[1]
    vt_ref[0:width, :] = v.T
    vt_ref[width:, :] = jnp.ones((vt_ref.shape[0] - width, vt_ref.shape[1]), BF16)


def _fill_keys(ka_ref, k_ref, one_lane):
    lane = lax.broadcasted_iota(jnp.int32, ka_ref.shape, 1)
    is_one = (lane == one_lane) | (lane == one_lane + 1)
    ka_ref[...] = jnp.where(is_one, jnp.ones((), BF16), k_ref[...])


def _diff_attn_kernel(lq1_ref, lk1_ref, lq2_ref, lk2_ref, subln_ref, q_ref, k_ref, v_ref, o_ref,
                      vt_ref, *, seq, tq):
    lam = (jnp.exp(jnp.sum(lq1_ref[...] * lk1_ref[...], axis=-1, keepdims=True))
           - jnp.exp(jnp.sum(lq2_ref[...] * lk2_ref[...], axis=-1, keepdims=True))
           + LAM_INIT)
    dv = DIFF_V_DIM
    _fill_vt(vt_ref, v_ref[...])

    def finish(lo, a):
        o_t = (a[0:dv, 0:tq] * (1.0 / a[dv:dv + 1, 0:tq])
               - a[0:dv, tq:] * (lam / a[dv:dv + 1, tq:]))
        o = _rmsnorm(o_t.T, subln_ref[...]) * (1.0 - LAM_INIT)
        o_ref[lo:lo + tq, :] = o.astype(BF16)

    _attend_tiles_online(k_ref, q_ref, _diff_first_map, vt_ref, seq, tq, finish)


def _mla_attn_kernel(q_ref, k_ref, v_ref, o_ref, vt_ref, ka_ref, corr_ref, *, seq, tq):
    dv = MLA_V_DIM
    _fill_vt(vt_ref, v_ref[...])
    _fill_keys(ka_ref, k_ref, MLA_REF_LANE)

    def finish(lo, a):
        o_t = jnp.concatenate(
            [a[0:dv, 0:tq] * (1.0 / a[2 * dv:2 * dv + 1, 0:tq]),
             a[dv:2 * dv, tq:] * (1.0 / a[2 * dv:2 * dv + 1, tq:])], axis=0)
        o_ref[lo:lo + tq, :] = o_t.T.astype(BF16)

    _attend_tiles_ref_point(ka_ref, q_ref, _mla_first_head, MLA_REF_LANE, vt_ref, corr_ref, seq, tq,
                            finish)


def _mlp_kernel(x_ref, od_ref, om_ref, p_ref, an_ref, wg_ref, bg_ref, wod_ref, wom_ref, wout_ref,
                fn_ref, wfg_ref, wfu_ref, wfd_ref, pn_ref, wpg_ref, bpg_ref, wple_ref, final_ref,
                o_ref):
    tm = x_ref.shape[0]
    groups = [slice(g * tm // MLP_ROW_GROUPS, (g + 1) * tm // MLP_ROW_GROUPS)
              for g in range(MLP_ROW_GROUPS)]

    def mix(r):
        x = x_ref[r, :]
        h = _rmsnorm(x, an_ref[...]).astype(BF16)
        gates = _sigmoid(_dot(h, wg_ref[...]) + bg_ref[...])
        out_a = _dot(od_ref[r, :], wod_ref[...])
        out_b = _dot(om_ref[r, :], wom_ref[...])
        merged = gates[:, :D_MODEL] * out_a + gates[:, D_MODEL:] * out_b
        return x + _dot(merged.astype(BF16), wout_ref[...])

    def ffn(x):
        h = _rmsnorm(x, fn_ref[...]).astype(BF16)
        gt = _dot(h, wfg_ref[...])
        up = _dot(h, wfu_ref[...])
        hid = (gt * _sigmoid(gt) * up).astype(BF16)
        return x + _dot(hid, wfd_ref[...])

    def embed(x, r):
        h = _rmsnorm(x, pn_ref[...]).astype(BF16)
        gate = _sigmoid(_dot(h, wpg_ref[...]) + bpg_ref[...])
        return x + _dot(p_ref[r, :].astype(BF16), wple_ref[...]) * gate

    xs = [mix(r) for r in groups]
    xs = [ffn(x) for x in xs]
    xs = [embed(x, r) for x, r in zip(xs, groups)]
    for x, r in zip(xs, groups):
        o_ref[r, :] = _rmsnorm(x, final_ref[...])


def _diff_lane_source():
    half, rest = DIFF_ROT_DIM // 2, (DIFF_HEAD_DIM - DIFF_ROT_DIM) // 2
    src = []
    for part in range(2):
        for m in range(2):
            src += [m * DIFF_HEAD_DIM + part * half + j for j in range(half)]
        for m in range(2):
            src += [m * DIFF_HEAD_DIM + DIFF_ROT_DIM + part * rest + j for j in range(rest)]
    return np.array(src)


def _mla_lane_sources():
    half = MLA_ROPE_DIM // 2
    first_nope = LANES // 2 - half
    nope = np.full(LANES, -1)
    rope = np.full(LANES, -1)
    rope[0:half] = np.arange(half)
    nope[half:LANES // 2] = np.arange(first_nope)
    rope[LANES // 2:LANES // 2 + half] = half + np.arange(half)
    rest = MLA_NOPE_DIM - first_nope
    nope[LANES // 2 + half:LANES // 2 + half + rest] = first_nope + np.arange(rest)
    return nope, rope


def _take_cols(w, idx):
    w = jnp.concatenate([w, jnp.zeros((w.shape[0], 1), w.dtype)], axis=1)
    return jnp.take(w, np.where(idx < 0, w.shape[1] - 1, idx), axis=1)


def _rope_tables(seq, rot_dim, theta, freq, passthrough, scale):
    pos = jnp.arange(seq, dtype=F32)
    inv_freq = theta ** (-(jnp.arange(0, rot_dim, 2, dtype=F32) / rot_dim))
    ang = pos[:, None] * inv_freq[None, :]
    cos_l = jnp.take(jnp.cos(ang), np.maximum(freq, 0), axis=1)
    sin_l = jnp.take(jnp.sin(ang), np.maximum(freq, 0), axis=1)
    sign = np.where(np.arange(LANES) < LANES // 2, -1.0, 1.0)
    c = jnp.where(freq >= 0, cos_l, passthrough.astype(np.float32))
    s = jnp.where(freq >= 0, sin_l * sign, 0.0)
    return jnp.stack([c, s]).astype(F32) * scale


def _const_spec(shape):
    nd = len(shape)
    return pl.BlockSpec(shape, lambda *_: (0,) * nd, pipeline_mode=pl.Buffered(1))


def _params(*sem):
    return pltpu.CompilerParams(dimension_semantics=sem, vmem_limit_bytes=VMEM_LIMIT_BYTES)


def kernel(x, p, attn_norm, w_in, b_gate, lam_q1, lam_k1, lam_q2, lam_k2, diff_subln, w_o_diff,
           q_norm, w_uq, kv_norm, w_ukv, w_o_mla, w_out, ffn_norm, w_ffn_gate, w_ffn_up,
           w_ffn_down, ple_norm, w_ple_gate, b_ple_gate, w_ple, final_norm):
    B, S, D = x.shape
    T = B * S
    assert D == D_MODEL and w_in.shape[0] == 1
    x2 = x.reshape(T, D)
    p2 = p[0].reshape(T, PLE_DIM)

    wi = w_in[0]
    o_cq = 3 * D
    o_kr = o_cq + MLA_Q_LORA + MLA_KV_LORA
    o_g = o_kr + MLA_ROPE_DIM
    heads = np.arange(DIFF_HEADS)[:, None]
    diff_src = _diff_lane_source()
    diff_cols = (heads * LANES + diff_src[None, :]).reshape(-1)
    w_qkv = jnp.concatenate([jnp.take(wi[:, :D], diff_cols, axis=1),
                             jnp.take(wi[:, D:2 * D], diff_cols, axis=1),
                             wi[:, 2 * D:o_cq]], axis=1).astype(BF16)
    nope_src, rope_src = _mla_lane_sources()
    w_c = jnp.concatenate([wi[:, o_cq:o_kr], _take_cols(wi[:, o_kr:o_g], rope_src)],
                          axis=1).astype(BF16)
    w_g = wi[:, o_g:].astype(BF16)
    b_g = b_gate[0].reshape(1, 2 * D)

    q_width = MLA_NOPE_DIM + MLA_ROPE_DIM
    q_src = np.where(nope_src >= 0, nope_src, np.where(rope_src >= 0, MLA_NOPE_DIM + rope_src, -1))
    q_cols = np.where(q_src[None, :] >= 0, heads * q_width + q_src[None, :], -1).reshape(-1)
    wuq = _take_cols(w_uq[0], q_cols).astype(BF16)
    kv_width = MLA_NOPE_DIM + MLA_V_DIM
    k_cols = np.where(nope_src[None, :] >= 0, heads * kv_width + nope_src[None, :], -1).reshape(-1)
    v_cols = (heads * kv_width + MLA_NOPE_DIM + np.arange(MLA_V_DIM)[None, :]).reshape(-1)
    wukv = _take_cols(w_ukv[0], np.concatenate([k_cols, v_cols])).astype(BF16)

    d_scale = DIFF_HEAD_DIM ** -0.5 * LOG2E
    m_scale = q_width ** -0.5 * LOG2E
    d_dim = diff_src % DIFF_HEAD_DIM
    d_freq = np.where(d_dim < DIFF_ROT_DIM, d_dim % (DIFF_ROT_DIM // 2), -1)
    ones = np.ones(LANES)
    dq_tab = _rope_tables(S, DIFF_ROT_DIM, ROPE_THETA, d_freq, ones, d_scale)
    dk_tab = _rope_tables(S, DIFF_ROT_DIM, ROPE_THETA, d_freq, ones, 1.0)
    m_freq = np.where(rope_src >= 0, rope_src % (MLA_ROPE_DIM // 2), -1)
    mq_tab = _rope_tables(S, MLA_ROPE_DIM, MLA_ROPE_THETA, m_freq, nope_src >= 0, m_scale)
    mk_tab = _rope_tables(S, MLA_ROPE_DIM, MLA_ROPE_THETA, m_freq, np.zeros(LANES), 1.0)

    row = lambda a: a.reshape(1, -1)

    tm = ROW_TILE_PROJ
    tiles_per_seq = S // tm
    tab_spec = pl.BlockSpec((2, tm, LANES), lambda i: (0, i % tiles_per_seq, 0))
    act = lambda n: pl.BlockSpec((tm, n), lambda i: (i, 0))
    dq, dk, dv, mq, mk, mv = pl.pallas_call(
        _proj_kernel,
        grid=(T // tm,),
        in_specs=[act(D), _const_spec((1, D)), _const_spec(w_qkv.shape), _const_spec(w_c.shape),
                  _const_spec((1, MLA_Q_LORA)), _const_spec(wuq.shape),
                  _const_spec((1, MLA_KV_LORA)), _const_spec(wukv.shape),
                  tab_spec, tab_spec, tab_spec, tab_spec],
        out_specs=[act(D), act(D), act(D), act(D), act(D), act(MLA_HEADS * MLA_V_DIM)],
        out_shape=[jax.ShapeDtypeStruct((T, D), BF16)] * 5
                  + [jax.ShapeDtypeStruct((T, MLA_HEADS * MLA_V_DIM), BF16)],
        compiler_params=_params("parallel"),
        name="proj",
    )(x2, row(attn_norm[0]), w_qkv, w_c, row(q_norm[0]), wuq, row(kv_norm[0]), wukv,
      dq_tab, dk_tab, mq_tab, mk_tab)

    head_blk = pl.BlockSpec((S, LANES), lambda b, h: (b, h))
    lam_spec = _const_spec((1, DIFF_HEAD_DIM))
    od = pl.pallas_call(
        functools.partial(_diff_attn_kernel, seq=S, tq=Q_TILE),
        grid=(B, DIFF_HEADS),
        in_specs=[lam_spec, lam_spec, lam_spec, lam_spec, _const_spec((1, DIFF_V_DIM)),
                  head_blk, head_blk, head_blk],
        out_specs=head_blk,
        out_shape=jax.ShapeDtypeStruct((T, DIFF_HEADS * DIFF_V_DIM), BF16),
        scratch_shapes=[pltpu.VMEM((DIFF_V_DIM + ONES_ROWS, S), BF16)],
        compiler_params=_params("parallel", "parallel"),
        name="diff_attn",
    )(lam_q1, lam_k1, lam_q2, lam_k2, row(diff_subln[0]), dq, dk, dv)

    pair_blk = pl.BlockSpec((S, 2 * LANES), lambda b, g: (b, g))
    om = pl.pallas_call(
        functools.partial(_mla_attn_kernel, seq=S, tq=Q_TILE),
        grid=(B, MLA_HEADS // 2),
        in_specs=[pair_blk, pair_blk, head_blk],
        out_specs=head_blk,
        out_shape=jax.ShapeDtypeStruct((T, MLA_HEADS * MLA_V_DIM), BF16),
        scratch_shapes=[pltpu.VMEM((2 * MLA_V_DIM + ONES_ROWS, S), BF16),
                        pltpu.VMEM((S, 2 * LANES), BF16),
                        pltpu.VMEM((S // Q_TILE, 8, 2 * Q_TILE), F32)],
        compiler_params=_params("parallel", "parallel"),
        name="mla_attn",
    )(mq, mk, mv)

    tm = ROW_TILE_MLP
    act = lambda n: pl.BlockSpec((tm, n), lambda i: (i, 0))
    weights = [w_g, b_g, w_o_diff[0].astype(BF16), w_o_mla[0].astype(BF16), w_out[0].astype(BF16),
               row(ffn_norm[0]), w_ffn_gate[0].astype(BF16), w_ffn_up[0].astype(BF16),
               w_ffn_down[0].astype(BF16), row(ple_norm[0]), w_ple_gate[0].astype(BF16),
               row(b_ple_gate[0]), w_ple[0].astype(BF16), row(final_norm)]
    out = pl.pallas_call(
        _mlp_kernel,
        grid=(T // tm,),
        in_specs=[act(D), act(D), act(MLA_HEADS * MLA_V_DIM), act(PLE_DIM), _const_spec((1, D))]
                 + [_const_spec(w.shape) for w in weights],
        out_specs=act(D),
        out_shape=jax.ShapeDtypeStruct((T, D), F32),
        compiler_params=_params("parallel"),
        name="mlp",
    )(x2, od, om, p2, row(attn_norm[0]), *weights)
    return out.reshape(B, S, D)
```

```python
import functools
import math

import jax
import jax.numpy as jnp
import numpy as np
from jax import lax
from jax.experimental import pallas as pl
from jax.experimental.pallas import tpu as pltpu

D_MODEL = 1024
CHUNK = 64
NORM_EPS = 1e-6
NEG_INF = -1e30

DIFF_HEAD_DIM = 64
DIFF_HEADS = 8
DIFF_V_DIM = 128
DIFF_ROT_DIM = 16
ROPE_THETA = 500000.0
LAM_INIT = 0.8 - 0.6 * math.exp(-0.3 * 0)

MLA_HEADS = 8
MLA_NOPE_DIM = 64
MLA_ROPE_DIM = 32
MLA_V_DIM = 64
MLA_Q_LORA = 384
MLA_KV_LORA = 256
MLA_ROPE_THETA = 10000.0

FFN_HIDDEN = 2816
PLE_DIM = 256

LANES = 128
ONES_ROWS = 16
LOG2E = math.log2(math.e)
VMEM_LIMIT_BYTES = 56 * 1024 * 1024

ROW_TILE_PROJ = 512
ROW_TILE_MLP = 512
MLP_ROW_GROUPS = 2
Q_TILE = 256
SCORE_LOOKAHEAD = 4
MLA_REF_LANE = MLA_NOPE_DIM + MLA_ROPE_DIM
REF_MARGIN = 100.0

BF16 = jnp.bfloat16
F32 = jnp.float32


def _rmsnorm(x, g):
    ms = jnp.mean(x * x, axis=-1, keepdims=True)
    return x * lax.rsqrt(ms + NORM_EPS) * g


def _dot(a, b):
    return jnp.dot(a, b, preferred_element_type=F32)


def _dot_nt(a, b):
    return lax.dot_general(a, b, (((1,), (1,)), ((), ())), preferred_element_type=F32)


def _sigmoid(x):
    return 1.0 / (1.0 + jnp.exp(-x))


def _rope_block(x, tab):
    return x * tab[0] + pltpu.roll(x, LANES // 2, 1) * tab[1]


def _proj_kernel(x_ref, an_ref, wqkv_ref, wc_ref, qn_ref, wuq_ref, kvn_ref, wukv_ref,
                 dq_tab, dk_tab, mq_tab, mk_tab,
                 dq_ref, dk_ref, dv_ref, mq_ref, mk_ref, mv_ref):
    h = _rmsnorm(x_ref[...], an_ref[...]).astype(BF16)

    c = _dot(h, wc_ref[...])
    cq = _rmsnorm(c[:, :MLA_Q_LORA], qn_ref[...]).astype(BF16)
    ckv = _rmsnorm(c[:, MLA_Q_LORA:MLA_Q_LORA + MLA_KV_LORA], kvn_ref[...]).astype(BF16)
    kr = _rope_block(c[:, MLA_Q_LORA + MLA_KV_LORA:], mk_tab)
    lane = lax.broadcasted_iota(jnp.int32, (1, LANES), 1)
    kr = kr + ((lane == MLA_REF_LANE) | (lane == MLA_REF_LANE + 1)).astype(F32)

    q = _dot(cq, wuq_ref[...])
    kv = _dot(ckv, wukv_ref[...])
    dq = _dot(h, wqkv_ref[:, 0:D_MODEL])
    for blk in range(MLA_HEADS):
        lo = blk * LANES
        mq_ref[:, lo:lo + LANES] = _rope_block(q[:, lo:lo + LANES], mq_tab).astype(BF16)
        mk_ref[:, lo:lo + LANES] = (kv[:, lo:lo + LANES] + kr).astype(BF16)
    mv_ref[...] = kv[:, MLA_HEADS * LANES:].astype(BF16)

    dk = _dot(h, wqkv_ref[:, D_MODEL:2 * D_MODEL])
    for blk in range(DIFF_HEADS):
        lo = blk * LANES
        dq_ref[:, lo:lo + LANES] = _rope_block(dq[:, lo:lo + LANES], dq_tab).astype(BF16)
    dv = _dot(h, wqkv_ref[:, 2 * D_MODEL:])
    for blk in range(DIFF_HEADS):
        lo = blk * LANES
        dk_ref[:, lo:lo + LANES] = _rope_block(dk[:, lo:lo + LANES], dk_tab).astype(BF16)
    dv_ref[...] = dv.astype(BF16)


def _pair_mask_t(tq):
    key = lax.broadcasted_iota(jnp.int32, (tq, 2 * tq), 0) // CHUNK
    qry = (lax.broadcasted_iota(jnp.int32, (tq, 2 * tq), 1) % tq) // CHUNK
    return key <= qry


def _block_diag_q(q, is_first):
    first = is_first(lax.broadcasted_iota(jnp.int32, q.shape, 1))
    zero = jnp.zeros((), q.dtype)
    return jnp.concatenate([jnp.where(first, q, zero), jnp.where(first, zero, q)], axis=0)


def _diff_first_map(lane):
    within = lane % (LANES // 2)
    half, rest = DIFF_ROT_DIM // 2, (DIFF_HEAD_DIM - DIFF_ROT_DIM) // 2
    return (within < half) | ((within >= 2 * half) & (within < 2 * half + rest))


def _mla_first_head(lane):
    return lane < LANES


def _attend_tiles_online(k_ref, q_ref, is_first, vt_ref, seq, tq, finish):
    n = seq // tq
    mask_t = _pair_mask_t(tq)
    chunks = [(t, kb) for t in range(n) for kb in range(t + 1)]
    q_bd = {}

    def scores(t, kb):
        if t not in q_bd:
            q_bd[t] = _block_diag_q(q_ref[t * tq:(t + 1) * tq, :], is_first)
        s = _dot_nt(k_ref[kb * tq:(kb + 1) * tq, :], q_bd[t])
        return jnp.where(mask_t, s, NEG_INF) if kb == t else s

    pending = [scores(*ch) for ch in chunks[:SCORE_LOOKAHEAD]]
    run_max = acc = None
    for c, (t, kb) in enumerate(chunks):
        s = pending.pop(0)
        m = jnp.max(s, axis=0, keepdims=True)
        new_max = m if kb == 0 else jnp.maximum(run_max, m)
        e = jnp.exp2(s - new_max).astype(BF16)
        part = _dot(vt_ref[:, kb * tq:(kb + 1) * tq], e)
        if c + SCORE_LOOKAHEAD < len(chunks):
            pending.append(scores(*chunks[c + SCORE_LOOKAHEAD]))
        acc = part if kb == 0 else acc * jnp.exp2(run_max - new_max) + part
        run_max = new_max
        if kb == t:
            finish(t * tq, acc)


def _attend_tiles_ref_point(ka_ref, q_ref, is_first, one_lane, vt_ref, corr_ref, seq, tq, finish):
    n = seq // tq
    mask_t = _pair_mask_t(tq)
    lane = lax.broadcasted_iota(jnp.int32, (2 * tq, ka_ref.shape[1]), 1)
    hi_lane, lo_lane = lane == one_lane, lane == one_lane + 1
    alt_rows = lax.broadcasted_iota(jnp.int32, (LANES, 2 * tq), 0) % 2 == 0

    def diag_stage(t):
        rows = slice(t * tq, (t + 1) * tq)
        q_bd = _block_diag_q(q_ref[rows, :], is_first)
        s_d = jnp.where(mask_t, _dot_nt(ka_ref[rows, :], q_bd), NEG_INF)
        m_d = jnp.max(s_d, axis=0, keepdims=True)
        base = m_d + corr_ref[t, 0:1, :]
        hi = base.astype(BF16).astype(F32)
        lo = (base - hi).astype(BF16).astype(F32)
        ref = hi + lo
        cols = jnp.where(alt_rows, hi, lo).T
        q_ref_pt = jnp.where(hi_lane, (-cols[:, 0:1]).astype(BF16),
                             jnp.where(lo_lane, (-cols[:, 1:2]).astype(BF16), q_bd))
        return s_d, ref, q_ref_pt, m_d - ref

    def one_pass():
        worst = None
        stages = [diag_stage(t) for t in range(n)]
        for t in range(n):
            s_d, ref, q_ref_pt, over = stages[t]
            e_all = jnp.exp2(s_d - ref).astype(BF16)
            if t > 0:
                s_rel = _dot_nt(ka_ref[0:t * tq, :], q_ref_pt)
                over = jnp.maximum(over, jnp.max(s_rel, axis=0, keepdims=True))
                e_all = jnp.concatenate([jnp.exp2(s_rel).astype(BF16), e_all], axis=0)
            finish(t * tq, _dot(vt_ref[:, 0:(t + 1) * tq], e_all))
            corr_ref[t] = jnp.broadcast_to(corr_ref[t, 0:1, :] + jnp.maximum(over, 0.0),
                                           corr_ref.shape[1:])
            worst = over if worst is None else jnp.maximum(worst, over)
        return (jnp.max(worst) > REF_MARGIN).astype(jnp.int32)

    corr_ref[...] = jnp.zeros(corr_ref.shape, F32)
    lax.while_loop(lambda c: (c[0] == 0) | ((c[0] == 1) & (c[1] > 0)),
                   lambda c: (c[0] + 1, one_pass()), (jnp.int32(0), jnp.int32(0)))


def _fill_vt(vt_ref, v):
    width = v.shape[1]
    vt_ref[0:width, :] = v.T
    vt_ref[width:, :] = jnp.ones((vt_ref.shape[0] - width, vt_ref.shape[1]), BF16)


def _diff_attn_kernel(lq1_ref, lk1_ref, lq2_ref, lk2_ref, subln_ref, q_ref, k_ref, v_ref, o_ref,
                      vt_ref, *, seq, tq):
    lam = (jnp.exp(jnp.sum(lq1_ref[...] * lk1_ref[...], axis=-1, keepdims=True))
           - jnp.exp(jnp.sum(lq2_ref[...] * lk2_ref[...], axis=-1, keepdims=True))
           + LAM_INIT)
    dv = DIFF_V_DIM
    _fill_vt(vt_ref, v_ref[...])

    def finish(lo, a):
        o_t = (a[0:dv, 0:tq] * (1.0 / a[dv:dv + 1, 0:tq])
               - a[0:dv, tq:] * (lam / a[dv:dv + 1, tq:]))
        o = _rmsnorm(o_t.T, subln_ref[...]) * (1.0 - LAM_INIT)
        o_ref[lo:lo + tq, :] = o.astype(BF16)

    _attend_tiles_online(k_ref, q_ref, _diff_first_map, vt_ref, seq, tq, finish)


def _mla_attn_kernel(q_ref, k_ref, v_ref, o_ref, vt_ref, corr_ref, *, seq, tq):
    dv = MLA_V_DIM
    _fill_vt(vt_ref, v_ref[...])

    def finish(lo, a):
        o_t = jnp.concatenate(
            [a[0:dv, 0:tq] * (1.0 / a[2 * dv:2 * dv + 1, 0:tq]),
             a[dv:2 * dv, tq:] * (1.0 / a[2 * dv:2 * dv + 1, tq:])], axis=0)
        o_ref[lo:lo + tq, :] = o_t.T.astype(BF16)

    _attend_tiles_ref_point(k_ref, q_ref, _mla_first_head, MLA_REF_LANE, vt_ref, corr_ref, seq, tq,
                            finish)


def _mlp_kernel(x_ref, od_ref, om_ref, p_ref, an_ref, wg_ref, bg_ref, wod_ref, wom_ref, wout_ref,
                fn_ref, wfg_ref, wfu_ref, wfd_ref, pn_ref, wpg_ref, bpg_ref, wple_ref, final_ref,
                o_ref):
    tm = x_ref.shape[0]
    groups = [slice(g * tm // MLP_ROW_GROUPS, (g + 1) * tm // MLP_ROW_GROUPS)
              for g in range(MLP_ROW_GROUPS)]

    def mix(r):
        x = x_ref[r, :]
        h = _rmsnorm(x, an_ref[...]).astype(BF16)
        gates = _sigmoid(_dot(h, wg_ref[...]) + bg_ref[...])
        out_a = _dot(od_ref[r, :], wod_ref[...])
        out_b = _dot(om_ref[r, :], wom_ref[...])
        merged = gates[:, :D_MODEL] * out_a + gates[:, D_MODEL:] * out_b
        return x + _dot(merged.astype(BF16), wout_ref[...])

    def ffn(x):
        h = _rmsnorm(x, fn_ref[...]).astype(BF16)
        gt = _dot(h, wfg_ref[...])
        up = _dot(h, wfu_ref[...])
        hid = (gt * _sigmoid(gt) * up).astype(BF16)
        return x + _dot(hid, wfd_ref[...])

    def embed(x, r):
        h = _rmsnorm(x, pn_ref[...]).astype(BF16)
        gate = _sigmoid(_dot(h, wpg_ref[...]) + bpg_ref[...])
        return x + _dot(p_ref[r, :].astype(BF16), wple_ref[...]) * gate

    xs = [mix(r) for r in groups]
    xs = [ffn(x) for x in xs]
    xs = [embed(x, r) for x, r in zip(xs, groups)]
    for x, r in zip(xs, groups):
        o_ref[r, :] = _rmsnorm(x, final_ref[...])


def _diff_lane_source():
    half, rest = DIFF_ROT_DIM // 2, (DIFF_HEAD_DIM - DIFF_ROT_DIM) // 2
    src = []
    for part in range(2):
        for m in range(2):
            src += [m * DIFF_HEAD_DIM + part * half + j for j in range(half)]
        for m in range(2):
            src += [m * DIFF_HEAD_DIM + DIFF_ROT_DIM + part * rest + j for j in range(rest)]
    return np.array(src)


def _mla_lane_sources():
    half = MLA_ROPE_DIM // 2
    first_nope = LANES // 2 - half
    nope = np.full(LANES, -1)
    rope = np.full(LANES, -1)
    rope[0:half] = np.arange(half)
    nope[half:LANES // 2] = np.arange(first_nope)
    rope[LANES // 2:LANES // 2 + half] = half + np.arange(half)
    rest = MLA_NOPE_DIM - first_nope
    nope[LANES // 2 + half:LANES // 2 + half + rest] = first_nope + np.arange(rest)
    return nope, rope


def _take_cols(w, idx):
    w = jnp.concatenate([w, jnp.zeros((w.shape[0], 1), w.dtype)], axis=1)
    return jnp.take(w, np.where(idx < 0, w.shape[1] - 1, idx), axis=1)


def _rope_tables(seq, rot_dim, theta, freq, passthrough, scale):
    pos = jnp.arange(seq, dtype=F32)
    inv_freq = theta ** (-(jnp.arange(0, rot_dim, 2, dtype=F32) / rot_dim))
    ang = pos[:, None] * inv_freq[None, :]
    cos_l = jnp.take(jnp.cos(ang), np.maximum(freq, 0), axis=1)
    sin_l = jnp.take(jnp.sin(ang), np.maximum(freq, 0), axis=1)
    sign = np.where(np.arange(LANES) < LANES // 2, -1.0, 1.0)
    c = jnp.where(freq >= 0, cos_l, passthrough.astype(np.float32))
    s = jnp.where(freq >= 0, sin_l * sign, 0.0)
    return jnp.stack([c, s]).astype(F32) * scale


def _const_spec(shape):
    nd = len(shape)
    return pl.BlockSpec(shape, lambda *_: (0,) * nd, pipeline_mode=pl.Buffered(1))


def _params(*sem):
    return pltpu.CompilerParams(dimension_semantics=sem, vmem_limit_bytes=VMEM_LIMIT_BYTES)


def kernel(x, p, attn_norm, w_in, b_gate, lam_q1, lam_k1, lam_q2, lam_k2, diff_subln, w_o_diff,
           q_norm, w_uq, kv_norm, w_ukv, w_o_mla, w_out, ffn_norm, w_ffn_gate, w_ffn_up,
           w_ffn_down, ple_norm, w_ple_gate, b_ple_gate, w_ple, final_norm):
    B, S, D = x.shape
    T = B * S
    assert D == D_MODEL and w_in.shape[0] == 1
    x2 = x.reshape(T, D)
    p2 = p[0].reshape(T, PLE_DIM)

    wi = w_in[0]
    o_cq = 3 * D
    o_kr = o_cq + MLA_Q_LORA + MLA_KV_LORA
    o_g = o_kr + MLA_ROPE_DIM
    heads = np.arange(DIFF_HEADS)[:, None]
    diff_src = _diff_lane_source()
    diff_cols = (heads * LANES + diff_src[None, :]).reshape(-1)
    w_qkv = jnp.concatenate([jnp.take(wi[:, :D], diff_cols, axis=1),
                             jnp.take(wi[:, D:2 * D], diff_cols, axis=1),
                             wi[:, 2 * D:o_cq]], axis=1).astype(BF16)
    nope_src, rope_src = _mla_lane_sources()
    w_c = jnp.concatenate([wi[:, o_cq:o_kr], _take_cols(wi[:, o_kr:o_g], rope_src)],
                          axis=1).astype(BF16)
    w_g = wi[:, o_g:].astype(BF16)
    b_g = b_gate[0].reshape(1, 2 * D)

    q_width = MLA_NOPE_DIM + MLA_ROPE_DIM
    q_src = np.where(nope_src >= 0, nope_src, np.where(rope_src >= 0, MLA_NOPE_DIM + rope_src, -1))
    q_cols = np.where(q_src[None, :] >= 0, heads * q_width + q_src[None, :], -1).reshape(-1)
    wuq = _take_cols(w_uq[0], q_cols).astype(BF16)
    kv_width = MLA_NOPE_DIM + MLA_V_DIM
    k_cols = np.where(nope_src[None, :] >= 0, heads * kv_width + nope_src[None, :], -1).reshape(-1)
    v_cols = (heads * kv_width + MLA_NOPE_DIM + np.arange(MLA_V_DIM)[None, :]).reshape(-1)
    wukv = _take_cols(w_ukv[0], np.concatenate([k_cols, v_cols])).astype(BF16)

    d_scale = DIFF_HEAD_DIM ** -0.5 * LOG2E
    m_scale = q_width ** -0.5 * LOG2E
    d_dim = diff_src % DIFF_HEAD_DIM
    d_freq = np.where(d_dim < DIFF_ROT_DIM, d_dim % (DIFF_ROT_DIM // 2), -1)
    ones = np.ones(LANES)
    dq_tab = _rope_tables(S, DIFF_ROT_DIM, ROPE_THETA, d_freq, ones, d_scale)
    dk_tab = _rope_tables(S, DIFF_ROT_DIM, ROPE_THETA, d_freq, ones, 1.0)
    m_freq = np.where(rope_src >= 0, rope_src % (MLA_ROPE_DIM // 2), -1)
    mq_tab = _rope_tables(S, MLA_ROPE_DIM, MLA_ROPE_THETA, m_freq, nope_src >= 0, m_scale)
    mk_tab = _rope_tables(S, MLA_ROPE_DIM, MLA_ROPE_THETA, m_freq, np.zeros(LANES), 1.0)

    row = lambda a: a.reshape(1, -1)

    tm = ROW_TILE_PROJ
    tiles_per_seq = S // tm
    tab_spec = pl.BlockSpec((2, tm, LANES), lambda i: (0, i % tiles_per_seq, 0))
    act = lambda n: pl.BlockSpec((tm, n), lambda i: (i, 0))
    dq, dk, dv, mq, mk, mv = pl.pallas_call(
        _proj_kernel,
        grid=(T // tm,),
        in_specs=[act(D), _const_spec((1, D)), _const_spec(w_qkv.shape), _const_spec(w_c.shape),
                  _const_spec((1, MLA_Q_LORA)), _const_spec(wuq.shape),
                  _const_spec((1, MLA_KV_LORA)), _const_spec(wukv.shape),
                  tab_spec, tab_spec, tab_spec, tab_spec],
        out_specs=[act(D), act(D), act(D), act(D), act(D), act(MLA_HEADS * MLA_V_DIM)],
        out_shape=[jax.ShapeDtypeStruct((T, D), BF16)] * 5
                  + [jax.ShapeDtypeStruct((T, MLA_HEADS * MLA_V_DIM), BF16)],
        compiler_params=_params("parallel"),
        name="proj",
    )(x2, row(attn_norm[0]), w_qkv, w_c, row(q_norm[0]), wuq, row(kv_norm[0]), wukv,
      dq_tab, dk_tab, mq_tab, mk_tab)

    head_blk = pl.BlockSpec((S, LANES), lambda b, h: (b, h))
    lam_spec = _const_spec((1, DIFF_HEAD_DIM))
    od = pl.pallas_call(
        functools.partial(_diff_attn_kernel, seq=S, tq=Q_TILE),
        grid=(B, DIFF_HEADS),
        in_specs=[lam_spec, lam_spec, lam_spec, lam_spec, _const_spec((1, DIFF_V_DIM)),
                  head_blk, head_blk, head_blk],
        out_specs=head_blk,
        out_shape=jax.ShapeDtypeStruct((T, DIFF_HEADS * DIFF_V_DIM), BF16),
        scratch_shapes=[pltpu.VMEM((DIFF_V_DIM + ONES_ROWS, S), BF16)],
        compiler_params=_params("parallel", "parallel"),
        name="diff_attn",
    )(lam_q1, lam_k1, lam_q2, lam_k2, row(diff_subln[0]), dq, dk, dv)

    pair_blk = pl.BlockSpec((S, 2 * LANES), lambda b, g: (b, g))
    om = pl.pallas_call(
        functools.partial(_mla_attn_kernel, seq=S, tq=Q_TILE),
        grid=(B, MLA_HEADS // 2),
        in_specs=[pair_blk, pair_blk, head_blk],
        out_specs=head_blk,
        out_shape=jax.ShapeDtypeStruct((T, MLA_HEADS * MLA_V_DIM), BF16),
        scratch_shapes=[pltpu.VMEM((2 * MLA_V_DIM + ONES_ROWS, S), BF16),
                        pltpu.VMEM((S // Q_TILE, 8, 2 * Q_TILE), F32)],
        compiler_params=_params("parallel", "parallel"),
        name="mla_attn",
    )(mq, mk, mv)

    tm = ROW_TILE_MLP
    act = lambda n: pl.BlockSpec((tm, n), lambda i: (i, 0))
    weights = [w_g, b_g, w_o_diff[0].astype(BF16), w_o_mla[0].astype(BF16), w_out[0].astype(BF16),
               row(ffn_norm[0]), w_ffn_gate[0].astype(BF16), w_ffn_up[0].astype(BF16),
               w_ffn_down[0].astype(BF16), row(ple_norm[0]), w_ple_gate[0].astype(BF16),
               row(b_ple_gate[0]), w_ple[0].astype(BF16), row(final_norm)]
    out = pl.pallas_call(
        _mlp_kernel,
        grid=(T // tm,),
        in_specs=[act(D), act(D), act(MLA_HEADS * MLA_V_DIM), act(PLE_DIM), _const_spec((1, D))]
                 + [_const_spec(w.shape) for w in weights],
        out_specs=act(D),
        out_shape=jax.ShapeDtypeStruct((T, D), F32),
        compiler_params=_params("parallel"),
        name="mlp",
    )(x2, od, om, p2, row(attn_norm[0]), *weights)
    return out.reshape(B, S, D)
```

```python
import functools
import math

import jax
import jax.numpy as jnp
import numpy as np
from jax import lax
from jax.experimental import pallas as pl
from jax.experimental.pallas import tpu as pltpu

D_MODEL = 1024
CHUNK = 64
NORM_EPS = 1e-6
NEG_INF = -1e30

DIFF_HEAD_DIM = 64
DIFF_HEADS = 8
DIFF_V_DIM = 128
DIFF_ROT_DIM = 16
ROPE_THETA = 500000.0
LAM_INIT = 0.8 - 0.6 * math.exp(-0.3 * 0)

MLA_HEADS = 8
MLA_NOPE_DIM = 64
MLA_ROPE_DIM = 32
MLA_V_DIM = 64
MLA_Q_LORA = 384
MLA_KV_LORA = 256
MLA_ROPE_THETA = 10000.0

FFN_HIDDEN = 2816
PLE_DIM = 256

LANES = 128
ONES_ROWS = 16
LOG2E = math.log2(math.e)
VMEM_LIMIT_BYTES = 56 * 1024 * 1024

ROW_TILE_PROJ = 512
ROW_TILE_MLP = 512
MLP_ROW_GROUPS = 2
Q_TILE = 256
SCORE_LOOKAHEAD = 4
MLA_REF_LANE = MLA_NOPE_DIM + MLA_ROPE_DIM
TILE_LOOKAHEAD = 2
REF_MARGIN = 100.0

BF16 = jnp.bfloat16
F32 = jnp.float32


def _rmsnorm(x, g):
    ms = jnp.mean(x * x, axis=-1, keepdims=True)
    return x * lax.rsqrt(ms + NORM_EPS) * g


def _dot(a, b):
    return jnp.dot(a, b, preferred_element_type=F32)


def _dot_nt(a, b):
    return lax.dot_general(a, b, (((1,), (1,)), ((), ())), preferred_element_type=F32)


def _sigmoid(x):
    return 1.0 / (1.0 + jnp.exp(-x))


def _rope_block(x, tab):
    return x * tab[0] + pltpu.roll(x, LANES // 2, 1) * tab[1]


def _proj_kernel(x_ref, an_ref, wqkv_ref, wc_ref, qn_ref, wuq_ref, kvn_ref, wukv_ref,
                 dq_tab, dk_tab, mq_tab, mk_tab,
                 dq_ref, dk_ref, dv_ref, mq_ref, mk_ref, mv_ref):
    h = _rmsnorm(x_ref[...], an_ref[...]).astype(BF16)

    c = _dot(h, wc_ref[...])
    cq = _rmsnorm(c[:, :MLA_Q_LORA], qn_ref[...]).astype(BF16)
    ckv = _rmsnorm(c[:, MLA_Q_LORA:MLA_Q_LORA + MLA_KV_LORA], kvn_ref[...]).astype(BF16)
    kr = _rope_block(c[:, MLA_Q_LORA + MLA_KV_LORA:], mk_tab)
    lane = lax.broadcasted_iota(jnp.int32, (1, LANES), 1)
    kr = kr + ((lane == MLA_REF_LANE) | (lane == MLA_REF_LANE + 1)).astype(F32)

    q = _dot(cq, wuq_ref[...])
    kv = _dot(ckv, wukv_ref[...])
    dq = _dot(h, wqkv_ref[:, 0:D_MODEL])
    for blk in range(MLA_HEADS):
        lo = blk * LANES
        mq_ref[:, lo:lo + LANES] = _rope_block(q[:, lo:lo + LANES], mq_tab).astype(BF16)
        mk_ref[:, lo:lo + LANES] = (kv[:, lo:lo + LANES] + kr).astype(BF16)
    mv_ref[...] = kv[:, MLA_HEADS * LANES:].astype(BF16)

    dk = _dot(h, wqkv_ref[:, D_MODEL:2 * D_MODEL])
    for blk in range(DIFF_HEADS):
        lo = blk * LANES
        dq_ref[:, lo:lo + LANES] = _rope_block(dq[:, lo:lo + LANES], dq_tab).astype(BF16)
    dv = _dot(h, wqkv_ref[:, 2 * D_MODEL:])
    for blk in range(DIFF_HEADS):
        lo = blk * LANES
        dk_ref[:, lo:lo + LANES] = _rope_block(dk[:, lo:lo + LANES], dk_tab).astype(BF16)
    dv_ref[...] = dv.astype(BF16)


def _pair_mask_t(tq):
    key = lax.broadcasted_iota(jnp.int32, (tq, 2 * tq), 0) // CHUNK
    qry = (lax.broadcasted_iota(jnp.int32, (tq, 2 * tq), 1) % tq) // CHUNK
    return key <= qry


def _block_diag_q(q, is_first):
    first = is_first(lax.broadcasted_iota(jnp.int32, q.shape, 1))
    zero = jnp.zeros((), q.dtype)
    return jnp.concatenate([jnp.where(first, q, zero), jnp.where(first, zero, q)], axis=0)


def _diff_first_map(lane):
    within = lane % (LANES // 2)
    half, rest = DIFF_ROT_DIM // 2, (DIFF_HEAD_DIM - DIFF_ROT_DIM) // 2
    return (within < half) | ((within >= 2 * half) & (within < 2 * half + rest))


def _mla_first_head(lane):
    return lane < LANES


def _attend_tiles_online(k_ref, q_ref, is_first, vt_ref, seq, tq, finish):
    n = seq // tq
    mask_t = _pair_mask_t(tq)
    chunks = [(t, kb) for t in range(n) for kb in range(t + 1)]
    q_bd = {}

    def scores(t, kb):
        if t not in q_bd:
            q_bd[t] = _block_diag_q(q_ref[t * tq:(t + 1) * tq, :], is_first)
        s = _dot_nt(k_ref[kb * tq:(kb + 1) * tq, :], q_bd[t])
        return jnp.where(mask_t, s, NEG_INF) if kb == t else s

    pending = [scores(*ch) for ch in chunks[:SCORE_LOOKAHEAD]]
    run_max = acc = None
    for c, (t, kb) in enumerate(chunks):
        s = pending.pop(0)
        m = jnp.max(s, axis=0, keepdims=True)
        new_max = m if kb == 0 else jnp.maximum(run_max, m)
        e = jnp.exp2(s - new_max).astype(BF16)
        part = _dot(vt_ref[:, kb * tq:(kb + 1) * tq], e)
        if c + SCORE_LOOKAHEAD < len(chunks):
            pending.append(scores(*chunks[c + SCORE_LOOKAHEAD]))
        acc = part if kb == 0 else acc * jnp.exp2(run_max - new_max) + part
        run_max = new_max
        if kb == t:
            finish(t * tq, acc)


def _attend_tiles_ref_point(ka_ref, q_ref, is_first, one_lane, vt_ref, corr_ref, seq, tq, finish):
    n = seq // tq
    mask_t = _pair_mask_t(tq)
    lane = lax.broadcasted_iota(jnp.int32, (2 * tq, ka_ref.shape[1]), 1)
    hi_lane, lo_lane = lane == one_lane, lane == one_lane + 1
    alt_rows = lax.broadcasted_iota(jnp.int32, (LANES, 2 * tq), 0) % 2 == 0

    def diag_stage(t):
        rows = slice(t * tq, (t + 1) * tq)
        q_bd = _block_diag_q(q_ref[rows, :], is_first)
        s_d = jnp.where(mask_t, _dot_nt(ka_ref[rows, :], q_bd), NEG_INF)
        m_d = jnp.max(s_d, axis=0, keepdims=True)
        base = m_d + corr_ref[t, 0:1, :]
        hi = base.astype(BF16).astype(F32)
        lo = (base - hi).astype(BF16).astype(F32)
        ref = hi + lo
        cols = jnp.where(alt_rows, hi, lo).T
        q_ref_pt = jnp.where(hi_lane, (-cols[:, 0:1]).astype(BF16),
                             jnp.where(lo_lane, (-cols[:, 1:2]).astype(BF16), q_bd))
        return s_d, ref, q_ref_pt, m_d - ref

    def one_pass():
        worst = None
        stages = [diag_stage(t) for t in range(n)]
        off = lambda t: _dot_nt(ka_ref[0:t * tq, :], stages[t][2])
        ahead = [off(t) for t in range(1, min(1 + TILE_LOOKAHEAD, n))]
        for t in range(n):
            s_d, ref, q_ref_pt, over = stages[t]
            e_all = jnp.exp2(s_d - ref).astype(BF16)
            if t > 0:
                s_rel = ahead.pop(0)
                if t + TILE_LOOKAHEAD < n:
                    ahead.append(off(t + TILE_LOOKAHEAD))
                over = jnp.maximum(over, jnp.max(s_rel, axis=0, keepdims=True))
                e_all = jnp.concatenate([jnp.exp2(s_rel).astype(BF16), e_all], axis=0)
            finish(t * tq, _dot(vt_ref[:, 0:(t + 1) * tq], e_all))
            corr_ref[t] = jnp.broadcast_to(corr_ref[t, 0:1, :] + jnp.maximum(over, 0.0),
                                           corr_ref.shape[1:])
            worst = over if worst is None else jnp.maximum(worst, over)
        return (jnp.max(worst) > REF_MARGIN).astype(jnp.int32)

    corr_ref[...] = jnp.zeros(corr_ref.shape, F32)
    lax.while_loop(lambda c: (c[0] == 0) | ((c[0] == 1) & (c[1] > 0)),
                   lambda c: (c[0] + 1, one_pass()), (jnp.int32(0), jnp.int32(0)))


def _fill_vt(vt_ref, v):
    width = v.shape[1]
    vt_ref[0:width, :] = v.T
    vt_ref[width:, :] = jnp.ones((vt_ref.shape[0] - width, vt_ref.shape[1]), BF16)


def _diff_attn_kernel(lq1_ref, lk1_ref, lq2_ref, lk2_ref, subln_ref, q_ref, k_ref, v_ref, o_ref,
                      vt_ref, *, seq, tq):
    lam = (jnp.exp(jnp.sum(lq1_ref[...] * lk1_ref[...], axis=-1, keepdims=True))
           - jnp.exp(jnp.sum(lq2_ref[...] * lk2_ref[...], axis=-1, keepdims=True))
           + LAM_INIT)
    dv = DIFF_V_DIM
    _fill_vt(vt_ref, v_ref[...])

    def finish(lo, a):
        o_t = (a[0:dv, 0:tq] * (1.0 / a[dv:dv + 1, 0:tq])
               - a[0:dv, tq:] * (lam / a[dv:dv + 1, tq:]))
        o = _rmsnorm(o_t.T, subln_ref[...]) * (1.0 - LAM_INIT)
        o_ref[lo:lo + tq, :] = o.astype(BF16)

    _attend_tiles_online(k_ref, q_ref, _diff_first_map, vt_ref, seq, tq, finish)


def _mla_attn_kernel(q_ref, k_ref, v_ref, o_ref, vt_ref, corr_ref, *, seq, tq):
    dv = MLA_V_DIM
    _fill_vt(vt_ref, v_ref[...])

    def finish(lo, a):
        o_t = jnp.concatenate(
            [a[0:dv, 0:tq] * (1.0 / a[2 * dv:2 * dv + 1, 0:tq]),
             a[dv:2 * dv, tq:] * (1.0 / a[2 * dv:2 * dv + 1, tq:])], axis=0)
        o_ref[lo:lo + tq, :] = o_t.T.astype(BF16)

    _attend_tiles_ref_point(k_ref, q_ref, _mla_first_head, MLA_REF_LANE, vt_ref, corr_ref, seq, tq,
                            finish)


def _mlp_kernel(x_ref, od_ref, om_ref, p_ref, an_ref, wg_ref, bg_ref, wod_ref, wom_ref, wout_ref,
                fn_ref, wfg_ref, wfu_ref, wfd_ref, pn_ref, wpg_ref, bpg_ref, wple_ref, final_ref,
                o_ref):
    tm = x_ref.shape[0]
    groups = [slice(g * tm // MLP_ROW_GROUPS, (g + 1) * tm // MLP_ROW_GROUPS)
              for g in range(MLP_ROW_GROUPS)]

    def mix(r):
        x = x_ref[r, :]
        h = _rmsnorm(x, an_ref[...]).astype(BF16)
        gates = _sigmoid(_dot(h, wg_ref[...]) + bg_ref[...])
        out_a = _dot(od_ref[r, :], wod_ref[...])
        out_b = _dot(om_ref[r, :], wom_ref[...])
        merged = gates[:, :D_MODEL] * out_a + gates[:, D_MODEL:] * out_b
        return x + _dot(merged.astype(BF16), wout_ref[...])

    def ffn(x):
        h = _rmsnorm(x, fn_ref[...]).astype(BF16)
        gt = _dot(h, wfg_ref[...])
        up = _dot(h, wfu_ref[...])
        hid = (gt * _sigmoid(gt) * up).astype(BF16)
        return x + _dot(hid, wfd_ref[...])

    def embed(x, r):
        h = _rmsnorm(x, pn_ref[...]).astype(BF16)
        gate = _sigmoid(_dot(h, wpg_ref[...]) + bpg_ref[...])
        return x + _dot(p_ref[r, :].astype(BF16), wple_ref[...]) * gate

    xs = [mix(r) for r in groups]
    xs = [ffn(x) for x in xs]
    xs = [embed(x, r) for x, r in zip(xs, groups)]
    for x, r in zip(xs, groups):
        o_ref[r, :] = _rmsnorm(x, final_ref[...])


def _diff_lane_source():
    half, rest = DIFF_ROT_DIM // 2, (DIFF_HEAD_DIM - DIFF_ROT_DIM) // 2
    src = []
    for part in range(2):
        for m in range(2):
            src += [m * DIFF_HEAD_DIM + part * half + j for j in range(half)]
        for m in range(2):
            src += [m * DIFF_HEAD_DIM + DIFF_ROT_DIM + part * rest + j for j in range(rest)]
    return np.array(src)


def _mla_lane_sources():
    half = MLA_ROPE_DIM // 2
    first_nope = LANES // 2 - half
    nope = np.full(LANES, -1)
    rope = np.full(LANES, -1)
    rope[0:half] = np.arange(half)
    nope[half:LANES // 2] = np.arange(first_nope)
    rope[LANES // 2:LANES // 2 + half] = half + np.arange(half)
    rest = MLA_NOPE_DIM - first_nope
    nope[LANES // 2 + half:LANES // 2 + half + rest] = first_nope + np.arange(rest)
    return nope, rope


def _take_cols(w, idx):
    w = jnp.concatenate([w, jnp.zeros((w.shape[0], 1), w.dtype)], axis=1)
    return jnp.take(w, np.where(idx < 0, w.shape[1] - 1, idx), axis=1)


def _rope_tables(seq, rot_dim, theta, freq, passthrough, scale):
    pos = jnp.arange(seq, dtype=F32)
    inv_freq = theta ** (-(jnp.arange(0, rot_dim, 2, dtype=F32) / rot_dim))
    ang = pos[:, None] * inv_freq[None, :]
    cos_l = jnp.take(jnp.cos(ang), np.maximum(freq, 0), axis=1)
    sin_l = jnp.take(jnp.sin(ang), np.maximum(freq, 0), axis=1)
    sign = np.where(np.arange(LANES) < LANES // 2, -1.0, 1.0)
    c = jnp.where(freq >= 0, cos_l, passthrough.astype(np.float32))
    s = jnp.where(freq >= 0, sin_l * sign, 0.0)
    return jnp.stack([c, s]).astype(F32) * scale


def _const_spec(shape):
    nd = len(shape)
    return pl.BlockSpec(shape, lambda *_: (0,) * nd, pipeline_mode=pl.Buffered(1))


def _params(*sem):
    return pltpu.CompilerParams(dimension_semantics=sem, vmem_limit_bytes=VMEM_LIMIT_BYTES)


def kernel(x, p, attn_norm, w_in, b_gate, lam_q1, lam_k1, lam_q2, lam_k2, diff_subln, w_o_diff,
           q_norm, w_uq, kv_norm, w_ukv, w_o_mla, w_out, ffn_norm, w_ffn_gate, w_ffn_up,
           w_ffn_down, ple_norm, w_ple_gate, b_ple_gate, w_ple, final_norm):
    B, S, D = x.shape
    T = B * S
    assert D == D_MODEL and w_in.shape[0] == 1
    x2 = x.reshape(T, D)
    p2 = p[0].reshape(T, PLE_DIM)

    wi = w_in[0]
    o_cq = 3 * D
    o_kr = o_cq + MLA_Q_LORA + MLA_KV_LORA
    o_g = o_kr + MLA_ROPE_DIM
    heads = np.arange(DIFF_HEADS)[:, None]
    diff_src = _diff_lane_source()
    diff_cols = (heads * LANES + diff_src[None, :]).reshape(-1)
    w_qkv = jnp.concatenate([jnp.take(wi[:, :D], diff_cols, axis=1),
                             jnp.take(wi[:, D:2 * D], diff_cols, axis=1),
                             wi[:, 2 * D:o_cq]], axis=1).astype(BF16)
    nope_src, rope_src = _mla_lane_sources()
    w_c = jnp.concatenate([wi[:, o_cq:o_kr], _take_cols(wi[:, o_kr:o_g], rope_src)],
                          axis=1).astype(BF16)
    w_g = wi[:, o_g:].astype(BF16)
    b_g = b_gate[0].reshape(1, 2 * D)

    q_width = MLA_NOPE_DIM + MLA_ROPE_DIM
    q_src = np.where(nope_src >= 0, nope_src, np.where(rope_src >= 0, MLA_NOPE_DIM + rope_src, -1))
    q_cols = np.where(q_src[None, :] >= 0, heads * q_width + q_src[None, :], -1).reshape(-1)
    wuq = _take_cols(w_uq[0], q_cols).astype(BF16)
    kv_width = MLA_NOPE_DIM + MLA_V_DIM
    k_cols = np.where(nope_src[None, :] >= 0, heads * kv_width + nope_src[None, :], -1).reshape(-1)
    v_cols = (heads * kv_width + MLA_NOPE_DIM + np.arange(MLA_V_DIM)[None, :]).reshape(-1)
    wukv = _take_cols(w_ukv[0], np.concatenate([k_cols, v_cols])).astype(BF16)

    d_scale = DIFF_HEAD_DIM ** -0.5 * LOG2E
    m_scale = q_width ** -0.5 * LOG2E
    d_dim = diff_src % DIFF_HEAD_DIM
    d_freq = np.where(d_dim < DIFF_ROT_DIM, d_dim % (DIFF_ROT_DIM // 2), -1)
    ones = np.ones(LANES)
    dq_tab = _rope_tables(S, DIFF_ROT_DIM, ROPE_THETA, d_freq, ones, d_scale)
    dk_tab = _rope_tables(S, DIFF_ROT_DIM, ROPE_THETA, d_freq, ones, 1.0)
    m_freq = np.where(rope_src >= 0, rope_src % (MLA_ROPE_DIM // 2), -1)
    mq_tab = _rope_tables(S, MLA_ROPE_DIM, MLA_ROPE_THETA, m_freq, nope_src >= 0, m_scale)
    mk_tab = _rope_tables(S, MLA_ROPE_DIM, MLA_ROPE_THETA, m_freq, np.zeros(LANES), 1.0)

    row = lambda a: a.reshape(1, -1)

    tm = ROW_TILE_PROJ
    tiles_per_seq = S // tm
    tab_spec = pl.BlockSpec((2, tm, LANES), lambda i: (0, i % tiles_per_seq, 0))
    act = lambda n: pl.BlockSpec((tm, n), lambda i: (i, 0))
    dq, dk, dv, mq, mk, mv = pl.pallas_call(
        _proj_kernel,
        grid=(T // tm,),
        in_specs=[act(D), _const_spec((1, D)), _const_spec(w_qkv.shape), _const_spec(w_c.shape),
                  _const_spec((1, MLA_Q_LORA)), _const_spec(wuq.shape),
                  _const_spec((1, MLA_KV_LORA)), _const_spec(wukv.shape),
                  tab_spec, tab_spec, tab_spec, tab_spec],
        out_specs=[act(D), act(D), act(D), act(D), act(D), act(MLA_HEADS * MLA_V_DIM)],
        out_shape=[jax.ShapeDtypeStruct((T, D), BF16)] * 5
                  + [jax.ShapeDtypeStruct((T, MLA_HEADS * MLA_V_DIM), BF16)],
        compiler_params=_params("parallel"),
        name="proj",
    )(x2, row(attn_norm[0]), w_qkv, w_c, row(q_norm[0]), wuq, row(kv_norm[0]), wukv,
      dq_tab, dk_tab, mq_tab, mk_tab)

    head_blk = pl.BlockSpec((S, LANES), lambda b, h: (b, h))
    lam_spec = _const_spec((1, DIFF_HEAD_DIM))
    od = pl.pallas_call(
        functools.partial(_diff_attn_kernel, seq=S, tq=Q_TILE),
        grid=(B, DIFF_HEADS),
        in_specs=[lam_spec, lam_spec, lam_spec, lam_spec, _const_spec((1, DIFF_V_DIM)),
                  head_blk, head_blk, head_blk],
        out_specs=head_blk,
        out_shape=jax.ShapeDtypeStruct((T, DIFF_HEADS * DIFF_V_DIM), BF16),
        scratch_shapes=[pltpu.VMEM((DIFF_V_DIM + ONES_ROWS, S), BF16)],
        compiler_params=_params("parallel", "parallel"),
        name="diff_attn",
    )(lam_q1, lam_k1, lam_q2, lam_k2, row(diff_subln[0]), dq, dk, dv)

    pair_blk = pl.BlockSpec((S, 2 * LANES), lambda b, g: (b, g))
    om = pl.pallas_call(
        functools.partial(_mla_attn_kernel, seq=S, tq=Q_TILE),
        grid=(B, MLA_HEADS // 2),
        in_specs=[pair_blk, pair_blk, head_blk],
        out_specs=head_blk,
        out_shape=jax.ShapeDtypeStruct((T, MLA_HEADS * MLA_V_DIM), BF16),
        scratch_shapes=[pltpu.VMEM((2 * MLA_V_DIM + ONES_ROWS, S), BF16),
                        pltpu.VMEM((S // Q_TILE, 8, 2 * Q_TILE), F32)],
        compiler_params=_params("parallel", "parallel"),
        name="mla_attn",
    )(mq, mk, mv)

    tm = ROW_TILE_MLP
    act = lambda n: pl.BlockSpec((tm, n), lambda i: (i, 0))
    weights = [w_g, b_g, w_o_diff[0].astype(BF16), w_o_mla[0].astype(BF16), w_out[0].astype(BF16),
               row(ffn_norm[0]), w_ffn_gate[0].astype(BF16), w_ffn_up[0].astype(BF16),
               w_ffn_down[0].astype(BF16), row(ple_norm[0]), w_ple_gate[0].astype(BF16),
               row(b_ple_gate[0]), w_ple[0].astype(BF16), row(final_norm)]
    out = pl.pallas_call(
        _mlp_kernel,
        grid=(T // tm,),
        in_specs=[act(D), act(D), act(MLA_HEADS * MLA_V_DIM), act(PLE_DIM), _const_spec((1, D))]
                 + [_const_spec(w.shape) for w in weights],
        out_specs=act(D),
        out_shape=jax.ShapeDtypeStruct((T, D), F32),
        compiler_params=_params("parallel"),
        name="mlp",
    )(x2, od, om, p2, row(attn_norm[0]), *weights)
    return out.reshape(B, S, D)
```

```python
import functools
import math

import jax
import jax.numpy as jnp
import numpy as np
from jax import lax
from jax.experimental import pallas as pl
from jax.experimental.pallas import tpu as pltpu

D_MODEL = 1024
CHUNK = 64
NORM_EPS = 1e-6
NEG_INF = -1e30

DIFF_HEAD_DIM = 64
DIFF_HEADS = 8
DIFF_V_DIM = 128
DIFF_ROT_DIM = 16
ROPE_THETA = 500000.0
LAM_INIT = 0.8 - 0.6 * math.exp(-0.3 * 0)

MLA_HEADS = 8
MLA_NOPE_DIM = 64
MLA_ROPE_DIM = 32
MLA_V_DIM = 64
MLA_Q_LORA = 384
MLA_KV_LORA = 256
MLA_ROPE_THETA = 10000.0

FFN_HIDDEN = 2816
PLE_DIM = 256

LANES = 128
ONES_ROWS = 16
LOG2E = math.log2(math.e)
VMEM_LIMIT_BYTES = 56 * 1024 * 1024

ROW_TILE_PROJ = 512
ROW_TILE_MLP = 512
MLP_ROW_GROUPS = 2
Q_TILE = 256
SCORE_LOOKAHEAD = 4
MLA_REF_LANE = MLA_NOPE_DIM + MLA_ROPE_DIM
TILE_LOOKAHEAD = 2
REF_MARGIN = 100.0

BF16 = jnp.bfloat16
F32 = jnp.float32


def _rmsnorm(x, g):
    ms = jnp.mean(x * x, axis=-1, keepdims=True)
    return x * lax.rsqrt(ms + NORM_EPS) * g


def _dot(a, b):
    return jnp.dot(a, b, preferred_element_type=F32)


def _dot_nt(a, b):
    return lax.dot_general(a, b, (((1,), (1,)), ((), ())), preferred_element_type=F32)


def _sigmoid(x):
    return 1.0 / (1.0 + jnp.exp(-x))


def _rope_block(x, tab):
    return x * tab[0] + pltpu.roll(x, LANES // 2, 1) * tab[1]


def _proj_kernel(x_ref, an_ref, wqkv_ref, wc_ref, qn_ref, wuq_ref, kvn_ref, wukv_ref,
                 dq_tab, dk_tab, mq_tab, mk_tab,
                 dq_ref, dk_ref, dv_ref, mq_ref, mk_ref, mv_ref):
    h = _rmsnorm(x_ref[...], an_ref[...]).astype(BF16)

    c = _dot(h, wc_ref[...])
    cq = _rmsnorm(c[:, :MLA_Q_LORA], qn_ref[...]).astype(BF16)
    ckv = _rmsnorm(c[:, MLA_Q_LORA:MLA_Q_LORA + MLA_KV_LORA], kvn_ref[...]).astype(BF16)
    kr = _rope_block(c[:, MLA_Q_LORA + MLA_KV_LORA:], mk_tab)
    lane = lax.broadcasted_iota(jnp.int32, (1, LANES), 1)
    kr = kr + ((lane == MLA_REF_LANE) | (lane == MLA_REF_LANE + 1)).astype(F32)

    q = _dot(cq, wuq_ref[...])
    kv = _dot(ckv, wukv_ref[...])
    dq = _dot(h, wqkv_ref[:, 0:D_MODEL])
    for blk in range(MLA_HEADS):
        lo = blk * LANES
        mq_ref[:, lo:lo + LANES] = _rope_block(q[:, lo:lo + LANES], mq_tab).astype(BF16)
        mk_ref[:, lo:lo + LANES] = (kv[:, lo:lo + LANES] + kr).astype(BF16)
    mv_ref[...] = kv[:, MLA_HEADS * LANES:].astype(BF16)

    dk = _dot(h, wqkv_ref[:, D_MODEL:2 * D_MODEL])
    for blk in range(DIFF_HEADS):
        lo = blk * LANES
        dq_ref[:, lo:lo + LANES] = _rope_block(dq[:, lo:lo + LANES], dq_tab).astype(BF16)
    dv = _dot(h, wqkv_ref[:, 2 * D_MODEL:])
    for blk in range(DIFF_HEADS):
        lo = blk * LANES
        dk_ref[:, lo:lo + LANES] = _rope_block(dk[:, lo:lo + LANES], dk_tab).astype(BF16)
    dv_ref[...] = dv.astype(BF16)


def _pair_mask_t(tq):
    key = lax.broadcasted_iota(jnp.int32, (tq, 2 * tq), 0) // CHUNK
    qry = (lax.broadcasted_iota(jnp.int32, (tq, 2 * tq), 1) % tq) // CHUNK
    return key <= qry


def _block_diag_q(q, is_first):
    first = is_first(lax.broadcasted_iota(jnp.int32, q.shape, 1))
    zero = jnp.zeros((), q.dtype)
    return jnp.concatenate([jnp.where(first, q, zero), jnp.where(first, zero, q)], axis=0)


def _diff_first_map(lane):
    within = lane % (LANES // 2)
    half, rest = DIFF_ROT_DIM // 2, (DIFF_HEAD_DIM - DIFF_ROT_DIM) // 2
    return (within < half) | ((within >= 2 * half) & (within < 2 * half + rest))


def _mla_first_head(lane):
    return lane < LANES


def _attend_tiles_online(k_ref, q_ref, is_first, vt_ref, seq, tq, finish):
    n = seq // tq
    mask_t = _pair_mask_t(tq)
    chunks = [(t, kb) for t in range(n) for kb in range(t + 1)]
    q_bd = {}

    def scores(t, kb):
        if t not in q_bd:
            q_bd[t] = _block_diag_q(q_ref[t * tq:(t + 1) * tq, :], is_first)
        s = _dot_nt(k_ref[kb * tq:(kb + 1) * tq, :], q_bd[t])
        return jnp.where(mask_t, s, NEG_INF) if kb == t else s

    pending = [scores(*ch) for ch in chunks[:SCORE_LOOKAHEAD]]
    run_max = acc = None
    for c, (t, kb) in enumerate(chunks):
        s = pending.pop(0)
        m = jnp.max(s, axis=0, keepdims=True)
        new_max = m if kb == 0 else jnp.maximum(run_max, m)
        e = jnp.exp2(s - new_max).astype(BF16)
        part = _dot(vt_ref[:, kb * tq:(kb + 1) * tq], e)
        if c + SCORE_LOOKAHEAD < len(chunks):
            pending.append(scores(*chunks[c + SCORE_LOOKAHEAD]))
        acc = part if kb == 0 else acc * jnp.exp2(run_max - new_max) + part
        run_max = new_max
        if kb == t:
            finish(t * tq, acc)


def _attend_tiles_ref_point(ka_ref, q_ref, is_first, one_lane, vt_ref, corr_ref, seq, tq, finish):
    n = seq // tq
    mask_t = _pair_mask_t(tq)
    lane = lax.broadcasted_iota(jnp.int32, (2 * tq, ka_ref.shape[1]), 1)
    hi_lane, lo_lane = lane == one_lane, lane == one_lane + 1
    alt_rows = lax.broadcasted_iota(jnp.int32, (LANES, 2 * tq), 0) % 2 == 0

    def diag_stage(t):
        rows = slice(t * tq, (t + 1) * tq)
        q_bd = _block_diag_q(q_ref[rows, :], is_first)
        s_d = jnp.where(mask_t, _dot_nt(ka_ref[rows, :], q_bd), NEG_INF)
        m_d = jnp.max(s_d, axis=0, keepdims=True)
        base = m_d + corr_ref[t, 0:1, :]
        hi = base.astype(BF16).astype(F32)
        lo = (base - hi).astype(BF16).astype(F32)
        ref = hi + lo
        cols = jnp.where(alt_rows, hi, lo).T
        q_ref_pt = jnp.where(hi_lane, (-cols[:, 0:1]).astype(BF16),
                             jnp.where(lo_lane, (-cols[:, 1:2]).astype(BF16), q_bd))
        return jnp.exp2(s_d - ref).astype(BF16), ref, q_ref_pt, m_d - ref

    def one_pass():
        worst = None
        stages = [diag_stage(t) for t in range(n)]
        off = lambda t: _dot_nt(ka_ref[0:t * tq, :], stages[t][2])
        ahead = [off(t) for t in range(1, min(1 + TILE_LOOKAHEAD, n))]
        for t in range(n):
            e_all, ref, q_ref_pt, over = stages[t]
            if t > 0:
                s_rel = ahead.pop(0)
                if t + TILE_LOOKAHEAD < n:
                    ahead.append(off(t + TILE_LOOKAHEAD))
                over = jnp.maximum(over, jnp.max(s_rel, axis=0, keepdims=True))
                e_all = jnp.concatenate([jnp.exp2(s_rel).astype(BF16), e_all], axis=0)
            finish(t * tq, _dot(vt_ref[:, 0:(t + 1) * tq], e_all))
            corr_ref[t] = jnp.broadcast_to(corr_ref[t, 0:1, :] + jnp.maximum(over, 0.0),
                                           corr_ref.shape[1:])
            worst = over if worst is None else jnp.maximum(worst, over)
        return (jnp.max(worst) > REF_MARGIN).astype(jnp.int32)

    corr_ref[...] = jnp.zeros(corr_ref.shape, F32)
    lax.while_loop(lambda c: (c[0] == 0) | ((c[0] == 1) & (c[1] > 0)),
                   lambda c: (c[0] + 1, one_pass()), (jnp.int32(0), jnp.int32(0)))


def _fill_vt(vt_ref, v):
    width = v.shape[1]
    vt_ref[0:width, :] = v.T
    vt_ref[width:, :] = jnp.ones((vt_ref.shape[0] - width, vt_ref.shape[1]), BF16)


def _diff_attn_kernel(lq1_ref, lk1_ref, lq2_ref, lk2_ref, subln_ref, q_ref, k_ref, v_ref, o_ref,
                      vt_ref, *, seq, tq):
    lam = (jnp.exp(jnp.sum(lq1_ref[...] * lk1_ref[...], axis=-1, keepdims=True))
           - jnp.exp(jnp.sum(lq2_ref[...] * lk2_ref[...], axis=-1, keepdims=True))
           + LAM_INIT)
    dv = DIFF_V_DIM
    _fill_vt(vt_ref, v_ref[...])

    def finish(lo, a):
        o_t = (a[0:dv, 0:tq] * (1.0 / a[dv:dv + 1, 0:tq])
               - a[0:dv, tq:] * (lam / a[dv:dv + 1, tq:]))
        o = _rmsnorm(o_t.T, subln_ref[...]) * (1.0 - LAM_INIT)
        o_ref[lo:lo + tq, :] = o.astype(BF16)

    _attend_tiles_online(k_ref, q_ref, _diff_first_map, vt_ref, seq, tq, finish)


def _mla_attn_kernel(q_ref, k_ref, v_ref, o_ref, vt_ref, corr_ref, *, seq, tq):
    dv = MLA_V_DIM
    _fill_vt(vt_ref, v_ref[...])

    def finish(lo, a):
        o_t = jnp.concatenate(
            [a[0:dv, 0:tq] * (1.0 / a[2 * dv:2 * dv + 1, 0:tq]),
             a[dv:2 * dv, tq:] * (1.0 / a[2 * dv:2 * dv + 1, tq:])], axis=0)
        o_ref[lo:lo + tq, :] = o_t.T.astype(BF16)

    _attend_tiles_ref_point(k_ref, q_ref, _mla_first_head, MLA_REF_LANE, vt_ref, corr_ref, seq, tq,
                            finish)


def _mlp_kernel(x_ref, od_ref, om_ref, p_ref, an_ref, wg_ref, bg_ref, wod_ref, wom_ref, wout_ref,
                fn_ref, wfg_ref, wfu_ref, wfd_ref, pn_ref, wpg_ref, bpg_ref, wple_ref, final_ref,
                o_ref):
    tm = x_ref.shape[0]
    groups = [slice(g * tm // MLP_ROW_GROUPS, (g + 1) * tm // MLP_ROW_GROUPS)
              for g in range(MLP_ROW_GROUPS)]

    def mix(r):
        x = x_ref[r, :]
        h = _rmsnorm(x, an_ref[...]).astype(BF16)
        gates = _sigmoid(_dot(h, wg_ref[...]) + bg_ref[...])
        out_a = _dot(od_ref[r, :], wod_ref[...])
        out_b = _dot(om_ref[r, :], wom_ref[...])
        merged = gates[:, :D_MODEL] * out_a + gates[:, D_MODEL:] * out_b
        return x + _dot(merged.astype(BF16), wout_ref[...])

    def ffn(x):
        h = _rmsnorm(x, fn_ref[...]).astype(BF16)
        gt = _dot(h, wfg_ref[...])
        up = _dot(h, wfu_ref[...])
        hid = (gt * _sigmoid(gt) * up).astype(BF16)
        return x + _dot(hid, wfd_ref[...])

    def embed(x, r):
        h = _rmsnorm(x, pn_ref[...]).astype(BF16)
        gate = _sigmoid(_dot(h, wpg_ref[...]) + bpg_ref[...])
        return x + _dot(p_ref[r, :].astype(BF16), wple_ref[...]) * gate

    xs = [mix(r) for r in groups]
    xs = [ffn(x) for x in xs]
    xs = [embed(x, r) for x, r in zip(xs, groups)]
    for x, r in zip(xs, groups):
        o_ref[r, :] = _rmsnorm(x, final_ref[...])


def _diff_lane_source():
    half, rest = DIFF_ROT_DIM // 2, (DIFF_HEAD_DIM - DIFF_ROT_DIM) // 2
    src = []
    for part in range(2):
        for m in range(2):
            src += [m * DIFF_HEAD_DIM + part * half + j for j in range(half)]
        for m in range(2):
            src += [m * DIFF_HEAD_DIM + DIFF_ROT_DIM + part * rest + j for j in range(rest)]
    return np.array(src)


def _mla_lane_sources():
    half = MLA_ROPE_DIM // 2
    first_nope = LANES // 2 - half
    nope = np.full(LANES, -1)
    rope = np.full(LANES, -1)
    rope[0:half] = np.arange(half)
    nope[half:LANES // 2] = np.arange(first_nope)
    rope[LANES // 2:LANES // 2 + half] = half + np.arange(half)
    rest = MLA_NOPE_DIM - first_nope
    nope[LANES // 2 + half:LANES // 2 + half + rest] = first_nope + np.arange(rest)
    return nope, rope


def _take_cols(w, idx):
    w = jnp.concatenate([w, jnp.zeros((w.shape[0], 1), w.dtype)], axis=1)
    return jnp.take(w, np.where(idx < 0, w.shape[1] - 1, idx), axis=1)


def _rope_tables(seq, rot_dim, theta, freq, passthrough, scale):
    pos = jnp.arange(seq, dtype=F32)
    inv_freq = theta ** (-(jnp.arange(0, rot_dim, 2, dtype=F32) / rot_dim))
    ang = pos[:, None] * inv_freq[None, :]
    cos_l = jnp.take(jnp.cos(ang), np.maximum(freq, 0), axis=1)
    sin_l = jnp.take(jnp.sin(ang), np.maximum(freq, 0), axis=1)
    sign = np.where(np.arange(LANES) < LANES // 2, -1.0, 1.0)
    c = jnp.where(freq >= 0, cos_l, passthrough.astype(np.float32))
    s = jnp.where(freq >= 0, sin_l * sign, 0.0)
    return jnp.stack([c, s]).astype(F32) * scale


def _const_spec(shape):
    nd = len(shape)
    return pl.BlockSpec(shape, lambda *_: (0,) * nd, pipeline_mode=pl.Buffered(1))


def _params(*sem):
    return pltpu.CompilerParams(dimension_semantics=sem, vmem_limit_bytes=VMEM_LIMIT_BYTES)


def kernel(x, p, attn_norm, w_in, b_gate, lam_q1, lam_k1, lam_q2, lam_k2, diff_subln, w_o_diff,
           q_norm, w_uq, kv_norm, w_ukv, w_o_mla, w_out, ffn_norm, w_ffn_gate, w_ffn_up,
           w_ffn_down, ple_norm, w_ple_gate, b_ple_gate, w_ple, final_norm):
    B, S, D = x.shape
    T = B * S
    assert D == D_MODEL and w_in.shape[0] == 1
    x2 = x.reshape(T, D)
    p2 = p[0].reshape(T, PLE_DIM)

    wi = w_in[0]
    o_cq = 3 * D
    o_kr = o_cq + MLA_Q_LORA + MLA_KV_LORA
    o_g = o_kr + MLA_ROPE_DIM
    heads = np.arange(DIFF_HEADS)[:, None]
    diff_src = _diff_lane_source()
    diff_cols = (heads * LANES + diff_src[None, :]).reshape(-1)
    w_qkv = jnp.concatenate([jnp.take(wi[:, :D], diff_cols, axis=1),
                             jnp.take(wi[:, D:2 * D], diff_cols, axis=1),
                             wi[:, 2 * D:o_cq]], axis=1).astype(BF16)
    nope_src, rope_src = _mla_lane_sources()
    w_c = jnp.concatenate([wi[:, o_cq:o_kr], _take_cols(wi[:, o_kr:o_g], rope_src)],
                          axis=1).astype(BF16)
    w_g = wi[:, o_g:].astype(BF16)
    b_g = b_gate[0].reshape(1, 2 * D)

    q_width = MLA_NOPE_DIM + MLA_ROPE_DIM
    q_src = np.where(nope_src >= 0, nope_src, np.where(rope_src >= 0, MLA_NOPE_DIM + rope_src, -1))
    q_cols = np.where(q_src[None, :] >= 0, heads * q_width + q_src[None, :], -1).reshape(-1)
    wuq = _take_cols(w_uq[0], q_cols).astype(BF16)
    kv_width = MLA_NOPE_DIM + MLA_V_DIM
    k_cols = np.where(nope_src[None, :] >= 0, heads * kv_width + nope_src[None, :], -1).reshape(-1)
    v_cols = (heads * kv_width + MLA_NOPE_DIM + np.arange(MLA_V_DIM)[None, :]).reshape(-1)
    wukv = _take_cols(w_ukv[0], np.concatenate([k_cols, v_cols])).astype(BF16)

    d_scale = DIFF_HEAD_DIM ** -0.5 * LOG2E
    m_scale = q_width ** -0.5 * LOG2E
    d_dim = diff_src % DIFF_HEAD_DIM
    d_freq = np.where(d_dim < DIFF_ROT_DIM, d_dim % (DIFF_ROT_DIM // 2), -1)
    ones = np.ones(LANES)
    dq_tab = _rope_tables(S, DIFF_ROT_DIM, ROPE_THETA, d_freq, ones, d_scale)
    dk_tab = _rope_tables(S, DIFF_ROT_DIM, ROPE_THETA, d_freq, ones, 1.0)
    m_freq = np.where(rope_src >= 0, rope_src % (MLA_ROPE_DIM // 2), -1)
    mq_tab = _rope_tables(S, MLA_ROPE_DIM, MLA_ROPE_THETA, m_freq, nope_src >= 0, m_scale)
    mk_tab = _rope_tables(S, MLA_ROPE_DIM, MLA_ROPE_THETA, m_freq, np.zeros(LANES), 1.0)

    row = lambda a: a.reshape(1, -1)

    tm = ROW_TILE_PROJ
    tiles_per_seq = S // tm
    tab_spec = pl.BlockSpec((2, tm, LANES), lambda i: (0, i % tiles_per_seq, 0))
    act = lambda n: pl.BlockSpec((tm, n), lambda i: (i, 0))
    dq, dk, dv, mq, mk, mv = pl.pallas_call(
        _proj_kernel,
        grid=(T // tm,),
        in_specs=[act(D), _const_spec((1, D)), _const_spec(w_qkv.shape), _const_spec(w_c.shape),
                  _const_spec((1, MLA_Q_LORA)), _const_spec(wuq.shape),
                  _const_spec((1, MLA_KV_LORA)), _const_spec(wukv.shape),
                  tab_spec, tab_spec, tab_spec, tab_spec],
        out_specs=[act(D), act(D), act(D), act(D), act(D), act(MLA_HEADS * MLA_V_DIM)],
        out_shape=[jax.ShapeDtypeStruct((T, D), BF16)] * 5
                  + [jax.ShapeDtypeStruct((T, MLA_HEADS * MLA_V_DIM), BF16)],
        compiler_params=_params("parallel"),
        name="proj",
    )(x2, row(attn_norm[0]), w_qkv, w_c, row(q_norm[0]), wuq, row(kv_norm[0]), wukv,
      dq_tab, dk_tab, mq_tab, mk_tab)

    head_blk = pl.BlockSpec((S, LANES), lambda b, h: (b, h))
    lam_spec = _const_spec((1, DIFF_HEAD_DIM))
    od = pl.pallas_call(
        functools.partial(_diff_attn_kernel, seq=S, tq=Q_TILE),
        grid=(B, DIFF_HEADS),
        in_specs=[lam_spec, lam_spec, lam_spec, lam_spec, _const_spec((1, DIFF_V_DIM)),
                  head_blk, head_blk, head_blk],
        out_specs=head_blk,
        out_shape=jax.ShapeDtypeStruct((T, DIFF_HEADS * DIFF_V_DIM), BF16),
        scratch_shapes=[pltpu.VMEM((DIFF_V_DIM + ONES_ROWS, S), BF16)],
        compiler_params=_params("parallel", "parallel"),
        name="diff_attn",
    )(lam_q1, lam_k1, lam_q2, lam_k2, row(diff_subln[0]), dq, dk, dv)

    pair_blk = pl.BlockSpec((S, 2 * LANES), lambda b, g: (b, g))
    om = pl.pallas_call(
        functools.partial(_mla_attn_kernel, seq=S, tq=Q_TILE),
        grid=(B, MLA_HEADS // 2),
        in_specs=[pair_blk, pair_blk, head_blk],
        out_specs=head_blk,
        out_shape=jax.ShapeDtypeStruct((T, MLA_HEADS * MLA_V_DIM), BF16),
        scratch_shapes=[pltpu.VMEM((2 * MLA_V_DIM + ONES_ROWS, S), BF16),
                        pltpu.VMEM((S // Q_TILE, 8, 2 * Q_TILE), F32)],
        compiler_params=_params("parallel", "parallel"),
        name="mla_attn",
    )(mq, mk, mv)

    tm = ROW_TILE_MLP
    act = lambda n: pl.BlockSpec((tm, n), lambda i: (i, 0))
    weights = [w_g, b_g, w_o_diff[0].astype(BF16), w_o_mla[0].astype(BF16), w_out[0].astype(BF16),
               row(ffn_norm[0]), w_ffn_gate[0].astype(BF16), w_ffn_up[0].astype(BF16),
               w_ffn_down[0].astype(BF16), row(ple_norm[0]), w_ple_gate[0].astype(BF16),
               row(b_ple_gate[0]), w_ple[0].astype(BF16), row(final_norm)]
    out = pl.pallas_call(
        _mlp_kernel,
        grid=(T // tm,),
        in_specs=[act(D), act(D), act(MLA_HEADS * MLA_V_DIM), act(PLE_DIM), _const_spec((1, D))]
                 + [_const_spec(w.shape) for w in weights],
        out_specs=act(D),
        out_shape=jax.ShapeDtypeStruct((T, D), F32),
        compiler_params=_params("parallel"),
        name="mlp",
    )(x2, od, om, p2, row(attn_norm[0]), *weights)
    return out.reshape(B, S, D)
```
